```python
import math
import jax, jax.numpy as jnp
from jax import lax
import numpy as np

D_MODEL = 4096
BATCH = 1
SEQ = 16384
DEPTH = 2

GRID_W = 64
CTX_LEN = 256
HEAD_DIM = 128
ROPE_THETA = 10000.0
EPS = 1e-6
Q_BLOCK = 128

MIX_WIDTH = D_MODEL
CONV_CH = MIX_WIDTH // 2
CONV_WIDTH = 31
DIFF_WIDTH = MIX_WIDTH - CONV_CH
DIFF_HEADS = DIFF_WIDTH // (2 * HEAD_DIM)
EVEN_IN = 2 * CONV_CH + 3 * DIFF_WIDTH
GQA_Q_HEADS = D_MODEL // HEAD_DIM
GQA_KV_HEADS = GQA_Q_HEADS // 4
GQA_GROUP = GQA_Q_HEADS // GQA_KV_HEADS
ODD_IN = (GQA_Q_HEADS + 2 * GQA_KV_HEADS) * HEAD_DIM
FFN_HIDDEN = 4 * D_MODEL

N_EVEN = (DEPTH + 1) // 2
N_ODD = DEPTH // 2

kernel_name = 'hybrid_conformer_diffattn_gqa_dit_block'


def rms_norm(x, g):
    xf = x.astype(jnp.float32)
    y = xf * lax.rsqrt(jnp.mean(xf * xf, axis=-1, keepdims=True) + EPS)
    return (y * g.astype(jnp.float32)).astype(x.dtype)


def layer_norm(x, g, b):
    xf = x.astype(jnp.float32)
    mu = jnp.mean(xf, axis=-1, keepdims=True)
    xc = xf - mu
    y = xc * lax.rsqrt(jnp.mean(xc * xc, axis=-1, keepdims=True) + EPS)
    return (y * g.astype(jnp.float32) + b.astype(jnp.float32)).astype(x.dtype)


def modulate(h, shift, scale):
    return h * (1 + scale) + shift


def ada_mod(cvec, w, b):
    m = jax.nn.silu(cvec) @ w + b
    return [t[..., None, :] for t in jnp.split(m, 6, axis=-1)]


def axial_rope_tables(rows):
    n_freq = HEAD_DIM // 4
    row = jnp.repeat(jnp.arange(rows, dtype=jnp.float32), GRID_W, total_repeat_length=rows * GRID_W)
    col = jnp.broadcast_to(jnp.arange(GRID_W, dtype=jnp.float32)[None, :], (rows, GRID_W)).reshape(-1)
    inv = ROPE_THETA ** (-jnp.arange(n_freq, dtype=jnp.float32) / n_freq)
    ang = jnp.concatenate([row[:, None] * inv, col[:, None] * inv], axis=-1)
    return jnp.cos(ang), jnp.sin(ang)


def apply_axial_rope(x, cos, sin):
    n_freq = HEAD_DIM // 4
    xf = x.astype(jnp.float32).reshape(x.shape[:-1] + (2, 2, n_freq))
    x1, x2 = xf[..., 0, :], xf[..., 1, :]
    bshape = (1, x.shape[1]) + (1,) * (x.ndim - 3) + (2, n_freq)
    cs, sn = cos.reshape(bshape), sin.reshape(bshape)
    out = jnp.stack([x1 * cs - x2 * sn, x2 * cs + x1 * sn], axis=-2)
    return out.reshape(x.shape).astype(x.dtype)


def sweep_query_blocks(fn, q):
    B, S = q.shape[:2]
    nb = S // Q_BLOCK
    qb = jnp.moveaxis(q.reshape((B, nb, Q_BLOCK) + q.shape[2:]), 1, 0)
    out = lax.map(fn, qb)
    return jnp.moveaxis(out, 0, 1).reshape((B, S) + out.shape[3:])


def diff_attn_core(q, k, v, lam, scale):
    s = jnp.einsum('bqhmd,bkhmd->bhmqk', q, k).astype(jnp.float32) * scale
    p = jax.nn.softmax(s, axis=-1)
    a = p[:, :, 0] - lam * p[:, :, 1]
    return jnp.einsum('bhqk,bkhe->bqhe', a.astype(v.dtype), v)


def gqa_core(q, k, v, scale):
    s = jnp.einsum('bqhgd,bkhd->bhgqk', q, k).astype(jnp.float32) * scale
    p = jax.nn.softmax(s, axis=-1)
    return jnp.einsum('bhgqk,bkhd->bqhgd', p.astype(v.dtype), v)


def depthwise_conv(u, w, b):
    y = lax.conv_general_dilated(u, w[:, None, :], window_strides=(1,), padding='SAME',
                                 dimension_numbers=('NWC', 'WIO', 'NWC'),
                                 feature_group_count=u.shape[-1])
    return y + b


def conformer_conv_branch(glu_val, glu_gate, conv_w, conv_b, ln_g, ln_b):
    u = glu_val * jax.nn.sigmoid(glu_gate)
    y = depthwise_conv(u, conv_w, conv_b)
    return jax.nn.silu(layer_norm(y, ln_g, ln_b))


def even_mixer(h_ctx, h_lat, cos, sin, w_in, conv_w, conv_b, ln_g, ln_b, q_g, k_g,
               lq1, lk1, lq2, lk2, subln_g, w_out, layer_idx, need_ctx):
    C = h_ctx.shape[1]
    proj = jnp.concatenate([h_ctx, h_lat], axis=1) @ w_in
    B, L = proj.shape[:2]
    cuts = [CONV_CH, 2 * CONV_CH, 2 * CONV_CH + DIFF_WIDTH, 2 * CONV_CH + 2 * DIFF_WIDTH]
    a_val, a_gate, q, k, v = jnp.split(proj, cuts, axis=-1)
    q = rms_norm(q.reshape(B, L, DIFF_HEADS, 2, HEAD_DIM), q_g)
    k = rms_norm(k.reshape(B, L, DIFF_HEADS, 2, HEAD_DIM), k_g)
    v = v.reshape(B, L, DIFF_HEADS, 2 * HEAD_DIM)
    lambda_init = 0.8 - 0.6 * math.exp(-0.3 * layer_idx)
    f32 = jnp.float32
    lam = (jnp.exp(jnp.sum(lq1.astype(f32) * lk1.astype(f32)))
           - jnp.exp(jnp.sum(lq2.astype(f32) * lk2.astype(f32))) + lambda_init)
    scale = HEAD_DIM ** -0.5
    k_ctx = k[:, :C]
    q_lat = apply_axial_rope(q[:, C:], cos, sin)
    k_all = jnp.concatenate([k_ctx, apply_axial_rope(k[:, C:], cos, sin)], axis=1)
    o_lat = sweep_query_blocks(lambda qb: diff_attn_core(qb, k_all, v, lam, scale), q_lat)

    def finish(o):
        o = rms_norm(o, subln_g) * (1 - lambda_init)
        return o.reshape(o.shape[0], o.shape[1], DIFF_WIDTH)

    a_lat = conformer_conv_branch(a_val[:, C:], a_gate[:, C:], conv_w, conv_b, ln_g, ln_b)
    y_lat = jnp.concatenate([a_lat, finish(o_lat)], axis=-1) @ w_out
    if not need_ctx:
        return None, y_lat
    o_ctx = diff_attn_core(q[:, :C], k_ctx, v[:, :C], lam, scale)
    a_ctx = conformer_conv_branch(a_val[:, :C], a_gate[:, :C], conv_w, conv_b, ln_g, ln_b)
    y_ctx = jnp.concatenate([a_ctx, finish(o_ctx)], axis=-1) @ w_out
    return y_ctx, y_lat


def odd_mixer(h_ctx, h_lat, cos, sin, w_in, q_g, k_g, w_out, need_ctx):
    C = h_ctx.shape[1]
    proj = jnp.concatenate([h_ctx, h_lat], axis=1) @ w_in
    B, L = proj.shape[:2]
    q, k, v = jnp.split(proj, [GQA_Q_HEADS * HEAD_DIM, (GQA_Q_HEADS + GQA_KV_HEADS) * HEAD_DIM], axis=-1)
    q = rms_norm(q.reshape(B, L, GQA_KV_HEADS, GQA_GROUP, HEAD_DIM), q_g)
    k = rms_norm(k.reshape(B, L, GQA_KV_HEADS, HEAD_DIM), k_g)
    v = v.reshape(B, L, GQA_KV_HEADS, HEAD_DIM)
    scale = HEAD_DIM ** -0.5
    k_ctx = k[:, :C]
    q_lat = apply_axial_rope(q[:, C:], cos, sin)
    k_all = jnp.concatenate([k_ctx, apply_axial_rope(k[:, C:], cos, sin)], axis=1)
    o_lat = sweep_query_blocks(lambda qb: gqa_core(qb, k_all, v, scale), q_lat)
    y_lat = o_lat.reshape(B, L - C, GQA_Q_HEADS * HEAD_DIM) @ w_out
    if not need_ctx:
        return None, y_lat
    o_ctx = gqa_core(q[:, :C], k_ctx, v[:, :C], scale)
    y_ctx = o_ctx.reshape(B, C, GQA_Q_HEADS * HEAD_DIM) @ w_out
    return y_ctx, y_lat


def sq_relu_mlp(h, w1, w2):
    return jnp.square(jax.nn.relu(h @ w1)) @ w2


def setup_inputs(seed: int = 0) -> dict:
    key = jax.random.key(seed)
    ks = jax.random.split(key, 32)
    f32 = jnp.float32
    D = D_MODEL

    def nrm(k, shape, scale):
        return jax.random.normal(k, shape, f32) * scale

    return {
        'x': nrm(ks[0], (BATCH, SEQ, D), 1.0),
        'c': nrm(ks[1], (BATCH, D), 1.0),
        'ctx': nrm(ks[2], (BATCH, CTX_LEN, D), 1.0),
        'c_ctx': nrm(ks[3], (D,), 1.0),
        'w_ada': nrm(ks[4], (DEPTH, D, 6 * D), 0.5 * D ** -0.5),
        'b_ada': nrm(ks[5], (DEPTH, 6 * D), 0.02),
        'norm_mix_g': 1.0 + nrm(ks[6], (DEPTH, D), 0.1),
        'norm_ffn_g': 1.0 + nrm(ks[7], (DEPTH, D), 0.1),
        'even_w_in': nrm(ks[8], (N_EVEN, D, EVEN_IN), D ** -0.5),
        'even_conv_w': nrm(ks[9], (N_EVEN, CONV_WIDTH, CONV_CH), CONV_WIDTH ** -0.5),
        'even_conv_b': nrm(ks[10], (N_EVEN, CONV_CH), 0.02),
        'even_ln_g': 1.0 + nrm(ks[11], (N_EVEN, CONV_CH), 0.1),
        'even_ln_b': nrm(ks[12], (N_EVEN, CONV_CH), 0.02),
        'even_q_norm_g': 1.0 + nrm(ks[13], (N_EVEN, HEAD_DIM), 0.1),
        'even_k_norm_g': 1.0 + nrm(ks[14], (N_EVEN, HEAD_DIM), 0.1),
        'even_lambda_q1': nrm(ks[15], (N_EVEN, HEAD_DIM), 0.1),
        'even_lambda_k1': nrm(ks[16], (N_EVEN, HEAD_DIM), 0.1),
        'even_lambda_q2': nrm(ks[17], (N_EVEN, HEAD_DIM), 0.1),
        'even_lambda_k2': nrm(ks[18], (N_EVEN, HEAD_DIM), 0.1),
        'even_subln_g': 1.0 + nrm(ks[19], (N_EVEN, 2 * HEAD_DIM), 0.1),
        'even_w_out': nrm(ks[20], (N_EVEN, MIX_WIDTH, D), MIX_WIDTH ** -0.5),
        'odd_w_in': nrm(ks[21], (N_ODD, D, ODD_IN), D ** -0.5),
        'odd_q_norm_g': 1.0 + nrm(ks[22], (N_ODD, HEAD_DIM), 0.1),
        'odd_k_norm_g': 1.0 + nrm(ks[23], (N_ODD, HEAD_DIM), 0.1),
        'odd_w_out': nrm(ks[24], (N_ODD, GQA_Q_HEADS * HEAD_DIM, D), (GQA_Q_HEADS * HEAD_DIM) ** -0.5),
        'w_ffn1': nrm(ks[25], (DEPTH, D, FFN_HIDDEN), D ** -0.5),
        'w_ffn2': nrm(ks[26], (DEPTH, FFN_HIDDEN, D), FFN_HIDDEN ** -0.5),
    }


def reference(x, c, ctx, c_ctx, w_ada, b_ada, norm_mix_g, norm_ffn_g,
              even_w_in, even_conv_w, even_conv_b, even_ln_g, even_ln_b,
              even_q_norm_g, even_k_norm_g, even_lambda_q1, even_lambda_k1,
              even_lambda_q2, even_lambda_k2, even_subln_g, even_w_out,
              odd_w_in, odd_q_norm_g, odd_k_norm_g, odd_w_out, w_ffn1, w_ffn2):
    ROWS = x.shape[1] // GRID_W
    cos, sin = axial_rope_tables(ROWS)
    x_lat, x_ctx = x, ctx
    for l in range(DEPTH):
        need_ctx = l < DEPTH - 1
        sm, cm, gm, sf, cf, gf = ada_mod(c, w_ada[l], b_ada[l])
        sm_c, cm_c, gm_c, sf_c, cf_c, gf_c = ada_mod(c_ctx, w_ada[l], b_ada[l])
        h_lat = modulate(rms_norm(x_lat, norm_mix_g[l]), sm, cm)
        h_ctx = modulate(rms_norm(x_ctx, norm_mix_g[l]), sm_c, cm_c)
        if l % 2 == 0:
            e = l // 2
            y_ctx, y_lat = even_mixer(h_ctx, h_lat, cos, sin, even_w_in[e], even_conv_w[e], even_conv_b[e],
                                      even_ln_g[e], even_ln_b[e], even_q_norm_g[e], even_k_norm_g[e],
                                      even_lambda_q1[e], even_lambda_k1[e], even_lambda_q2[e],
                                      even_lambda_k2[e], even_subln_g[e], even_w_out[e], l, need_ctx)
        else:
            o = l // 2
            y_ctx, y_lat = odd_mixer(h_ctx, h_lat, cos, sin, odd_w_in[o], odd_q_norm_g[o],
                                     odd_k_norm_g[o], odd_w_out[o], need_ctx)
        x_lat = x_lat + gm * y_lat
        h_lat = modulate(rms_norm(x_lat, norm_ffn_g[l]), sf, cf)
        x_lat = x_lat + gf * sq_relu_mlp(h_lat, w_ffn1[l], w_ffn2[l])
        if need_ctx:
            x_ctx = x_ctx + gm_c * y_ctx
            h_ctx = modulate(rms_norm(x_ctx, norm_ffn_g[l]), sf_c, cf_c)
            x_ctx = x_ctx + gf_c * sq_relu_mlp(h_ctx, w_ffn1[l], w_ffn2[l])
    return x_lat
```

```python
import functools
import math

import jax
import jax.numpy as jnp
from jax import lax
from jax.experimental import pallas as pl
from jax.experimental.pallas import tpu as pltpu

HEAD_DIM = 128
GRID_W = 64
ROPE_THETA = 10000.0
EPS = 1e-6

LANES = 128
SUBLANES = 8
V7X_VMEM_BYTES = 64 * 2**20
VMEM_CAP_BYTES = V7X_VMEM_BYTES - 6 * 2**20
CONV_HALO = 16

F32 = jnp.float32
BF16 = jnp.bfloat16
LOG2E = math.log2(math.e)


def _tile(n, pref, mult):
    if n <= pref:
        return n
    t = (pref // mult) * mult
    while t >= mult:
        if n % t == 0:
            return t
        t -= mult
    raise ValueError(f"no tile for {n} (pref {pref}, mult {mult})")


def _params(n_axes, est_bytes):
    limit = int(min(max(est_bytes * 1.25 + 4 * 2**20, 16 * 2**20), VMEM_CAP_BYTES))
    return pltpu.CompilerParams(dimension_semantics=("arbitrary",) * n_axes, vmem_limit_bytes=limit)


def _nbytes(shape, dtype):
    return math.prod(shape) * jnp.dtype(dtype).itemsize


def _ada_kernel(a_ref, w_ref, b_ref, o_ref, act_ref):
    nv = a_ref.shape[0]
    tn = w_ref.shape[1]

    @pl.when((pl.program_id(0) == 0) & (pl.program_id(1) == 0))
    def _():
        a = a_ref[...]
        act_ref[...] = a * jax.nn.sigmoid(a)

    rows = []
    for v in range(nv):
        parts = []
        for t in range(tn // LANES):
            prod = w_ref[:, t * LANES:(t + 1) * LANES] * act_ref[v]
            parts.append(jnp.sum(prod, axis=0, keepdims=True))
        rows.append(jnp.concatenate(parts, axis=1) + b_ref[...])
    rows.append(jnp.zeros((SUBLANES - nv, tn), F32))
    o_ref[...] = jnp.concatenate(rows, axis=0)


def _ada_mod(cvecs, w_ada, b_ada):
    nv, d = cvecs.shape
    depth, _, n = w_ada.shape
    assert nv < SUBLANES
    tn = _tile(n, 512, LANES)
    a_rep = jnp.broadcast_to(cvecs[:, :, None], (nv, d, LANES))
    est = 2 * (_nbytes((d, tn), F32) + _nbytes((nv, d, LANES), F32)) + _nbytes((nv, d, LANES), F32)
    return pl.pallas_call(
        _ada_kernel,
        out_shape=jax.ShapeDtypeStruct((depth, SUBLANES, n), F32),
        grid=(depth, n // tn),
        in_specs=[
            pl.BlockSpec((nv, d, LANES), lambda l, j: (0, 0, 0)),
            pl.BlockSpec((None, d, tn), lambda l, j: (l, 0, j)),
            pl.BlockSpec((None, 1, tn), lambda l, j: (l, 0, j)),
        ],
        out_specs=pl.BlockSpec((None, SUBLANES, tn), lambda l, j: (l, 0, j)),
        scratch_shapes=[pltpu.VMEM((nv, d, LANES), F32)],
        compiler_params=_params(2, est),
        name="ada_mod",
    )(a_rep, w_ada, b_ada.reshape(depth, 1, n))


def _norm_mod_kernel(x_ref, g_ref, scale_ref, shift_ref, o_ref):
    x = x_ref[...]
    ms = jnp.mean(x * x, axis=-1, keepdims=True)
    y = x * lax.rsqrt(ms + EPS) * g_ref[...]
    o_ref[...] = (y * (1.0 + scale_ref[...]) + shift_ref[...]).astype(o_ref.dtype)


def _norm_mod(x, g, scale, shift):
    m, d = x.shape
    tm = _tile(m, 256, SUBLANES)
    vec = pl.BlockSpec((1, d), lambda i: (0, 0))
    est = 2 * (_nbytes((tm, d), F32) + _nbytes((tm, d), BF16)) + 2 * _nbytes((tm, d), F32)
    return pl.pallas_call(
        _norm_mod_kernel,
        out_shape=jax.ShapeDtypeStruct((m, d), BF16),
        grid=(m // tm,),
        in_specs=[pl.BlockSpec((tm, d), lambda i: (i, 0)), vec, vec, vec],
        out_specs=pl.BlockSpec((tm, d), lambda i: (i, 0)),
        compiler_params=_params(1, est),
        name="norm_mod",
    )(x, g.reshape(1, d), scale, shift)


def _dot(a, b):
    return jnp.dot(a, b, preferred_element_type=F32)


def _rope_rotate(y, cos, sin_up, sin_dn):
    return y * cos + pltpu.roll(y, 96, 1) * sin_up + pltpu.roll(y, 32, 1) * sin_dn


def _mm_kernel(*refs, n_a, n_w, nk, epilogue, rope):
    a_refs = refs[:n_a]
    w_refs = refs[n_a:n_a + n_w]
    rest = refs[n_a + n_w:]

    if epilogue == "glu":
        val = _dot(a_refs[0][...], w_refs[0][...])
        gate = _dot(a_refs[0][...], w_refs[1][...])
        o_ref = rest[0]
        o_ref[...] = val * jax.nn.sigmoid(gate)
        return

    acc = _dot(a_refs[0][...], w_refs[0][...])
    for p in range(1, n_a):
        acc = acc + _dot(a_refs[p][...], w_refs[p][...])

    if epilogue == "plain":
        rest[0][...] = acc.astype(rest[0].dtype)
    elif epilogue == "sqrelu":
        r = jnp.maximum(acc, 0.0)
        rest[0][...] = (r * r).astype(rest[0].dtype)
    elif epilogue == "qk":
        if rope:
            g_ref, cos_ref, sup_ref, sdn_ref, o_ref = rest
        else:
            g_ref, o_ref = rest
        tn = acc.shape[1]
        for hh in range(tn // HEAD_DIM):
            sl = slice(hh * HEAD_DIM, (hh + 1) * HEAD_DIM)
            blk = acc[:, sl]
            ms = jnp.mean(blk * blk, axis=-1, keepdims=True)
            y = blk * lax.rsqrt(ms + EPS) * g_ref[:, sl]
            if rope:
                y = _rope_rotate(y, cos_ref[...], sup_ref[...], sdn_ref[...])
            o_ref[:, sl] = y.astype(o_ref.dtype)
    elif epilogue == "resid":
        if nk == 1:
            x_ref, gate_ref, o_ref = rest
            o_ref[...] = x_ref[...] + gate_ref[...] * acc
        else:
            x_ref, gate_ref, o_ref, acc_ref = rest
            k = pl.program_id(2)

            @pl.when(k == 0)
            def _():
                acc_ref[...] = acc

            @pl.when((k > 0) & (k < nk - 1))
            def _():
                acc_ref[...] += acc

            @pl.when(k == nk - 1)
            def _():
                o_ref[...] = x_ref[...] + gate_ref[...] * (acc_ref[...] + acc)
    else:
        raise ValueError(epilogue)


def _matmul(a_list, w_list, w_offsets, n_out, epilogue, extras=(), rope=False, out_dtype=BF16,
            tm_pref=1024, tn_pref=1024, tk_pref=4096, name="mm"):
    m, kdim = a_list[0].shape
    n_a, n_w = len(a_list), len(w_list)
    tm = _tile(m, tm_pref, SUBLANES)
    tn = _tile(n_out, tn_pref, LANES)
    tk = _tile(kdim, tk_pref, LANES)
    nk = kdim // tk
    assert nk == 1 or epilogue == "resid"
    grid = (m // tm, n_out // tn, nk)

    in_specs, est = [], 0
    for _ in a_list:
        in_specs.append(pl.BlockSpec((tm, tk), lambda i, j, k: (i, k)))
        est += 2 * _nbytes((tm, tk), BF16)
    for (ro, co) in w_offsets:
        assert ro % tk == 0 and co % tn == 0
        in_specs.append(pl.BlockSpec((tk, tn), functools.partial(
            lambda i, j, k, rb, cb: (k + rb, j + cb), rb=ro // tk, cb=co // tn)))
        est += 2 * _nbytes((tk, tn), BF16)
    operands = list(a_list) + list(w_list)

    col_vec = pl.BlockSpec((1, tn), lambda i, j, k: (0, j))
    row_tab = pl.BlockSpec((tm, HEAD_DIM), lambda i, j, k: (i, 0))
    tile = pl.BlockSpec((tm, tn), lambda i, j, k: (i, j))
    scratch = []
    if epilogue == "qk":
        in_specs += [col_vec] + ([row_tab] * 3 if rope else [])
        est += 6 * _nbytes((tm, HEAD_DIM), F32)
    elif epilogue == "resid":
        in_specs += [tile, col_vec]
        est += 2 * _nbytes((tm, tn), F32)
        if nk > 1:
            scratch.append(pltpu.VMEM((tm, tn), F32))
            est += _nbytes((tm, tn), F32)
    operands += list(extras)
    est += 2 * _nbytes((tm, tn), out_dtype) + 3 * _nbytes((tm, tn), F32)

    return pl.pallas_call(
        functools.partial(_mm_kernel, n_a=n_a, n_w=n_w, nk=nk, epilogue=epilogue, rope=rope),
        out_shape=jax.ShapeDtypeStruct((m, n_out), out_dtype),
        grid=grid,
        in_specs=in_specs,
        out_specs=tile,
        scratch_shapes=scratch,
        compiler_params=_params(3, est),
        name=name,
    )(*operands)


def _conv_kernel(prev_ref, cur_ref, next_ref, w_ref, cb_ref, g_ref, b_ref, o_ref, ext_ref, y_ref,
                 *, width, rb, cw):
    tm, ch = cur_ref.shape
    half = (width - 1) // 2
    i = pl.program_id(0)
    n = pl.num_programs(0)
    ext_ref[0:CONV_HALO, :] = jnp.where(i > 0, prev_ref[...], 0.0)
    ext_ref[CONV_HALO:CONV_HALO + tm, :] = cur_ref[...]
    ext_ref[CONV_HALO + tm:, :] = jnp.where(i < n - 1, next_ref[...], 0.0)

    def col_body(c, carry):
        c0 = pl.multiple_of(c * cw, cw)
        cols = pl.ds(c0, cw)
        taps = [w_ref[t:t + 1, cols] for t in range(width)]
        bias = cb_ref[:, cols]
        for r in range(tm // rb):
            acc = jnp.zeros((rb, cw), F32) + bias
            for t in range(width):
                acc = acc + ext_ref[pl.ds(r * rb + CONV_HALO - half + t, rb), cols] * taps[t]
            y_ref[r * rb:(r + 1) * rb, cols] = acc
        return carry

    lax.fori_loop(0, ch // cw, col_body, 0)

    y = y_ref[...]
    mu = jnp.mean(y, axis=-1, keepdims=True)
    yc = y - mu
    var = jnp.mean(yc * yc, axis=-1, keepdims=True)
    z = yc * lax.rsqrt(var + EPS) * g_ref[...] + b_ref[...]
    o_ref[...] = (z * jax.nn.sigmoid(z)).astype(o_ref.dtype)


def _conv_branch(u, conv_w, conv_b, ln_g, ln_b):
    m, ch = u.shape
    width = conv_w.shape[0]
    assert (width - 1) // 2 <= CONV_HALO and m % CONV_HALO == 0
    tm = _tile(m, 256, CONV_HALO)
    hb = tm // CONV_HALO
    last = m // CONV_HALO - 1
    vec = pl.BlockSpec((1, ch), lambda i: (0, 0))
    est = (2 * (_nbytes((tm + 2 * CONV_HALO, ch), F32) + _nbytes((tm, ch), BF16))
           + _nbytes((tm + 2 * CONV_HALO, ch), F32) + 4 * _nbytes((tm, ch), F32))
    return pl.pallas_call(
        functools.partial(_conv_kernel, width=width, rb=_tile(tm, 64, SUBLANES), cw=LANES),
        out_shape=jax.ShapeDtypeStruct((m, ch), BF16),
        grid=(m // tm,),
        in_specs=[
            pl.BlockSpec((CONV_HALO, ch), lambda i: (jnp.maximum(i * hb - 1, 0), 0)),
            pl.BlockSpec((tm, ch), lambda i: (i, 0)),
            pl.BlockSpec((CONV_HALO, ch), lambda i: (jnp.minimum((i + 1) * hb, last), 0)),
            pl.BlockSpec((width, ch), lambda i: (0, 0)),
            vec, vec, vec,
        ],
        out_specs=pl.BlockSpec((tm, ch), lambda i: (i, 0)),
        scratch_shapes=[pltpu.VMEM((tm + 2 * CONV_HALO, ch), F32), pltpu.VMEM((tm, ch), F32)],
        compiler_params=_params(1, est),
        name="conv_branch",
    )(u, u, u, conv_w, conv_b.reshape(1, ch), ln_g.reshape(1, ch), ln_b.reshape(1, ch))


def _softmax_step(q, k, v, m, l, acc_ref, c):
    s = lax.dot_general(q, k, (((1,), (1,)), ((), ())), preferred_element_type=F32)
    m_new = jnp.maximum(m, jnp.max(s, axis=-1, keepdims=True))
    alpha = jnp.exp2((m - m_new) * c)
    p = jnp.exp2((s - m_new) * c)
    l_new = alpha * l + jnp.sum(p, axis=-1, keepdims=True)
    acc_ref[...] = alpha * acc_ref[...] + _dot(p.astype(v.dtype), v)
    return m_new, l_new


def _diff_attn_kernel(*refs, n_src, chunks, lam_init):
    q_ref = refs[0]
    kv_refs = refs[1:1 + 2 * n_src]
    lamv_ref, subg_ref, o_ref, acc0_ref, acc1_ref = refs[1 + 2 * n_src:]
    d = HEAD_DIM
    tq = q_ref.shape[0]
    c = (d ** -0.5) * LOG2E
    q0 = q_ref[:, 0:d]
    q1 = q_ref[:, d:2 * d]
    acc0_ref[...] = jnp.zeros_like(acc0_ref)
    acc1_ref[...] = jnp.zeros_like(acc1_ref)
    neg = jnp.full((tq, 1), -jnp.inf, F32)
    zero = jnp.zeros((tq, 1), F32)
    state = (neg, zero, neg, zero)

    for s_idx in range(n_src):
        k_ref, v_ref = kv_refs[2 * s_idx], kv_refs[2 * s_idx + 1]
        tk = chunks[s_idx]

        def body(ci, st, k_ref=k_ref, v_ref=v_ref, tk=tk):
            m0, l0, m1, l1 = st
            rows = pl.ds(pl.multiple_of(ci * tk, tk), tk)
            v = v_ref[rows, :]
            m0, l0 = _softmax_step(q0, k_ref[rows, 0:d], v, m0, l0, acc0_ref, c)
            m1, l1 = _softmax_step(q1, k_ref[rows, d:2 * d], v, m1, l1, acc1_ref, c)
            return (m0, l0, m1, l1)

        state = lax.fori_loop(0, k_ref.shape[0] // tk, body, state)

    _, l0, _, l1 = state
    lv = lamv_ref[...]
    lam = (jnp.exp(jnp.sum(lv[0:1] * lv[1:2], axis=-1, keepdims=True))
           - jnp.exp(jnp.sum(lv[2:3] * lv[3:4], axis=-1, keepdims=True)) + lam_init)
    o = acc0_ref[...] / l0 - lam * (acc1_ref[...] / l1)
    ms = jnp.mean(o * o, axis=-1, keepdims=True)
    o = o * lax.rsqrt(ms + EPS) * subg_ref[...] * (1.0 - lam_init)
    o_ref[...] = o.astype(o_ref.dtype)


def _diff_attn(q_arr, q_col0, sources, lamv, subg, n_heads, lam_init, tq_pref=512, tk_pref=512):
    lq = q_arr.shape[0]
    d2 = 2 * HEAD_DIM
    tq = _tile(lq, tq_pref, SUBLANES)
    qb = q_col0 // d2
    in_specs = [pl.BlockSpec((tq, d2), lambda h, i: (i, h + qb))]
    operands, chunks = [q_arr], []
    est = 4 * _nbytes((tq, d2), BF16) + 2 * _nbytes((tq, d2), F32)
    for (k_arr, k_col0, v_arr) in sources:
        lk = k_arr.shape[0]
        kb = k_col0 // d2
        in_specs.append(pl.BlockSpec((lk, d2), functools.partial(lambda h, i, kb: (0, h + kb), kb=kb)))
        in_specs.append(pl.BlockSpec((lk, d2), lambda h, i: (0, h)))
        operands += [k_arr, v_arr]
        tk = _tile(lk, tk_pref, LANES)
        chunks.append(tk)
        est += 4 * _nbytes((lk, d2), BF16) + 8 * _nbytes((tq, tk), F32)
    in_specs += [pl.BlockSpec((4, HEAD_DIM), lambda h, i: (0, 0)), pl.BlockSpec((1, d2), lambda h, i: (0, 0))]
    operands += [lamv, subg.reshape(1, d2)]
    return pl.pallas_call(
        functools.partial(_diff_attn_kernel, n_src=len(sources), chunks=tuple(chunks), lam_init=lam_init),
        out_shape=jax.ShapeDtypeStruct((lq, n_heads * d2), BF16),
        grid=(n_heads, lq // tq),
        in_specs=in_specs,
        out_specs=pl.BlockSpec((tq, d2), lambda h, i: (i, h)),
        scratch_shapes=[pltpu.VMEM((tq, d2), F32), pltpu.VMEM((tq, d2), F32)],
        compiler_params=_params(2, est),
        name="diff_attn",
    )(*operands)


def _gqa_attn_kernel(*refs, n_src, chunks, group):
    q_ref = refs[0]
    kv_refs = refs[1:1 + 2 * n_src]
    o_ref, acc_ref = refs[1 + 2 * n_src:]
    d = HEAD_DIM
    tq = q_ref.shape[0]
    c = (d ** -0.5) * LOG2E
    q = jnp.concatenate([q_ref[:, g * d:(g + 1) * d] for g in range(group)], axis=0)
    acc_ref[...] = jnp.zeros_like(acc_ref)
    state = (jnp.full((group * tq, 1), -jnp.inf, F32), jnp.zeros((group * tq, 1), F32))

    for s_idx in range(n_src):
        k_ref, v_ref = kv_refs[2 * s_idx], kv_refs[2 * s_idx + 1]
        tk = chunks[s_idx]

        def body(ci, st, k_ref=k_ref, v_ref=v_ref, tk=tk):
            rows = pl.ds(pl.multiple_of(ci * tk, tk), tk)
            return _softmax_step(q, k_ref[rows, :], v_ref[rows, :], st[0], st[1], acc_ref, c)

        state = lax.fori_loop(0, k_ref.shape[0] // tk, body, state)

    o = acc_ref[...] / state[1]
    for g in range(group):
        o_ref[:, g * d:(g + 1) * d] = o[g * tq:(g + 1) * tq].astype(o_ref.dtype)


def _gqa_attn(q_arr, sources, n_kv, group, tq_pref=256, tk_pref=512):
    lq = q_arr.shape[0]
    d = HEAD_DIM
    gd = group * d
    tq = _tile(lq, tq_pref, SUBLANES)
    in_specs = [pl.BlockSpec((tq, gd), lambda h, i: (i, h))]
    operands, chunks = [q_arr], []
    est = 6 * _nbytes((tq, gd), BF16) + 2 * _nbytes((group * tq, d), F32)
    for (k_arr, k_col0, v_arr) in sources:
        lk = k_arr.shape[0]
        kb = k_col0 // d
        in_specs.append(pl.BlockSpec((lk, d), functools.partial(lambda h, i, kb: (0, h + kb), kb=kb)))
        in_specs.append(pl.BlockSpec((lk, d), lambda h, i: (0, h)))
        operands += [k_arr, v_arr]
        tk = _tile(lk, tk_pref, LANES)
        chunks.append(tk)
        est += 4 * _nbytes((lk, d), BF16) + 8 * _nbytes((group * tq, tk), F32)
    return pl.pallas_call(
        functools.partial(_gqa_attn_kernel, n_src=len(sources), chunks=tuple(chunks), group=group),
        out_shape=jax.ShapeDtypeStruct((lq, n_kv * gd), BF16),
        grid=(n_kv, lq // tq),
        in_specs=in_specs,
        out_specs=pl.BlockSpec((tq, gd), lambda h, i: (i, h)),
        scratch_shapes=[pltpu.VMEM((group * tq, d), F32)],
        compiler_params=_params(2, est),
        name="gqa_attn",
    )(*operands)


def _rope_tables(rows):
    n_freq = HEAD_DIM // 4
    tok = jnp.arange(rows * GRID_W, dtype=jnp.int32)
    row = (tok // GRID_W).astype(F32)
    col = (tok % GRID_W).astype(F32)
    inv = ROPE_THETA ** (-jnp.arange(n_freq, dtype=F32) / n_freq)
    cr, sr = jnp.cos(row[:, None] * inv), jnp.sin(row[:, None] * inv)
    cc, sc = jnp.cos(col[:, None] * inv), jnp.sin(col[:, None] * inv)
    z = jnp.zeros_like(sr)
    cos = jnp.concatenate([cr, cr, cc, cc], axis=-1)
    sin_up = jnp.concatenate([-sr, z, -sc, z], axis=-1)
    sin_dn = jnp.concatenate([z, sr, z, sc], axis=-1)
    return cos, sin_up, sin_dn


def _ffn(x, mods, g, w1, w2):
    sf, cf, gf = mods
    h = _norm_mod(x, g, cf, sf)
    hid = _matmul([h], [w1], [(0, 0)], w1.shape[1], "sqrelu", name="ffn_up")
    return _matmul([hid], [w2], [(0, 0)], w2.shape[1], "resid", extras=(x, gf), out_dtype=F32,
                   tk_pref=2048, name="ffn_down")


def _even_mixer(h_lat, h_ctx, x_lat, x_ctx, gm_lat, gm_ctx, tables, w_in, conv_w, conv_b, ln_g, ln_b,
                q_g, k_g, lamv, subg, w_out, layer_idx, need_ctx):
    d_model = h_lat.shape[1]
    conv_ch = conv_w.shape[1]
    diff_w = d_model - conv_ch
    n_heads = diff_w // (2 * HEAD_DIM)
    lam_init = 0.8 - 0.6 * math.exp(-0.3 * layer_idx)
    q_off = 2 * conv_ch
    g_vec = jnp.concatenate([jnp.tile(q_g, diff_w // HEAD_DIM), jnp.tile(k_g, diff_w // HEAD_DIM)]).reshape(1, -1)

    def project(h, rope_tabs):
        u = _matmul([h], [w_in, w_in], [(0, 0), (0, conv_ch)], conv_ch, "glu", out_dtype=F32,
                    tn_pref=512, name="in_glu")
        qk = _matmul([h], [w_in], [(0, q_off)], 2 * diff_w, "qk", extras=(g_vec,) + tuple(rope_tabs),
                     rope=bool(rope_tabs), name="in_qk")
        v = _matmul([h], [w_in], [(0, q_off + 2 * diff_w)], diff_w, "plain", name="in_v")
        return u, qk, v

    def finish(u, o, x, gm):
        a = _conv_branch(u, conv_w, conv_b, ln_g, ln_b)
        return _matmul([a, o], [w_out, w_out], [(0, 0), (conv_ch, 0)], d_model, "resid",
                       extras=(x, gm), out_dtype=F32, name="out_proj")

    u_lat, qk_lat, v_lat = project(h_lat, tables)
    u_ctx, qk_ctx, v_ctx = project(h_ctx, ())
    o_lat = _diff_attn(qk_lat, 0, [(qk_lat, diff_w, v_lat), (qk_ctx, diff_w, v_ctx)], lamv, subg,
                       n_heads, lam_init)
    x_lat = finish(u_lat, o_lat, x_lat, gm_lat)
    if need_ctx:
        o_ctx = _diff_attn(qk_ctx, 0, [(qk_ctx, diff_w, v_ctx)], lamv, subg, n_heads, lam_init)
        x_ctx = finish(u_ctx, o_ctx, x_ctx, gm_ctx)
    return x_lat, x_ctx


def _odd_mixer(h_lat, h_ctx, x_lat, x_ctx, gm_lat, gm_ctx, tables, w_in, q_g, k_g, w_out, need_ctx):
    d_model = h_lat.shape[1]
    n_q = d_model // HEAD_DIM
    n_kv = (w_in.shape[1] // HEAD_DIM - n_q) // 2
    group = n_q // n_kv
    q_w, kv_w = n_q * HEAD_DIM, n_kv * HEAD_DIM
    g_vec = jnp.concatenate([jnp.tile(q_g, n_q), jnp.tile(k_g, n_kv)]).reshape(1, -1)

    def project(h, rope_tabs, with_q):
        c0 = 0 if with_q else q_w
        qk = _matmul([h], [w_in], [(0, c0)], q_w + kv_w - c0, "qk", extras=(g_vec[:, c0:],) + tuple(rope_tabs),
                     rope=bool(rope_tabs), name="in_qk")
        v = _matmul([h], [w_in], [(0, q_w + kv_w)], kv_w, "plain", name="in_v")
        return qk, v

    qk_lat, v_lat = project(h_lat, tables, True)
    qk_ctx, v_ctx = project(h_ctx, (), need_ctx)
    k_ctx_col = q_w if need_ctx else 0
    o_lat = _gqa_attn(qk_lat, [(qk_lat, q_w, v_lat), (qk_ctx, k_ctx_col, v_ctx)], n_kv, group)
    x_lat = _matmul([o_lat], [w_out], [(0, 0)], d_model, "resid", extras=(x_lat, gm_lat), out_dtype=F32,
                    name="out_proj")
    if need_ctx:
        o_ctx = _gqa_attn(qk_ctx, [(qk_ctx, q_w, v_ctx)], n_kv, group)
        x_ctx = _matmul([o_ctx], [w_out], [(0, 0)], d_model, "resid", extras=(x_ctx, gm_ctx), out_dtype=F32,
                        name="out_proj")
    return x_lat, x_ctx


def kernel(x, c, ctx, c_ctx, w_ada, b_ada, norm_mix_g, norm_ffn_g, even_w_in, even_conv_w, even_conv_b,
           even_ln_g, even_ln_b, even_q_norm_g, even_k_norm_g, even_lambda_q1, even_lambda_k1,
           even_lambda_q2, even_lambda_k2, even_subln_g, even_w_out, odd_w_in, odd_q_norm_g, odd_k_norm_g,
           odd_w_out, w_ffn1, w_ffn2):
    batch, seq, d_model = x.shape
    depth = w_ada.shape[0]
    assert seq % GRID_W == 0
    tables = _rope_tables(seq // GRID_W)

    mods = _ada_mod(jnp.concatenate([c, c_ctx[None, :]], axis=0), w_ada, b_ada)
    mods = mods.reshape(depth, SUBLANES, 6, 1, d_model)

    bf = lambda w: w.astype(BF16)
    even_w_in, even_w_out, odd_w_in, odd_w_out = bf(even_w_in), bf(even_w_out), bf(odd_w_in), bf(odd_w_out)
    w_ffn1, w_ffn2 = bf(w_ffn1), bf(w_ffn2)

    outs = []
    for b in range(batch):
        x_lat, x_ctx = x[b], ctx[b]
        for l in range(depth):
            need_ctx = l < depth - 1
            sm, cm, gm, sf, cf, gf = (mods[l, b, t] for t in range(6))
            sm_c, cm_c, gm_c, sf_c, cf_c, gf_c = (mods[l, batch, t] for t in range(6))
            h_lat = _norm_mod(x_lat, norm_mix_g[l], cm, sm)
            h_ctx = _norm_mod(x_ctx, norm_mix_g[l], cm_c, sm_c)
            if l % 2 == 0:
                e = l // 2
                lamv = jnp.stack([even_lambda_q1[e], even_lambda_k1[e], even_lambda_q2[e], even_lambda_k2[e]])
                x_lat, x_ctx_new = _even_mixer(
                    h_lat, h_ctx, x_lat, x_ctx, gm, gm_c, tables, even_w_in[e], even_conv_w[e], even_conv_b[e],
                    even_ln_g[e], even_ln_b[e], even_q_norm_g[e], even_k_norm_g[e], lamv, even_subln_g[e],
                    even_w_out[e], l, need_ctx)
            else:
                o = l // 2
                x_lat, x_ctx_new = _odd_mixer(h_lat, h_ctx, x_lat, x_ctx, gm, gm_c, tables, odd_w_in[o],
                                              odd_q_norm_g[o], odd_k_norm_g[o], odd_w_out[o], need_ctx)
            x_lat = _ffn(x_lat, (sf, cf, gf), norm_ffn_g[l], w_ffn1[l], w_ffn2[l])
            if need_ctx:
                x_ctx = _ffn(x_ctx_new, (sf_c, cf_c, gf_c), norm_ffn_g[l], w_ffn1[l], w_ffn2[l])
        outs.append(x_lat)
    return jnp.stack(outs, axis=0)
```

```python
import functools
import math

import jax
import jax.numpy as jnp
from jax import lax
from jax.experimental import pallas as pl
from jax.experimental.pallas import tpu as pltpu

HEAD_DIM = 128
GRID_W = 64
ROPE_THETA = 10000.0
EPS = 1e-6

LANES = 128
SUBLANES = 8
V7X_VMEM_BYTES = 64 * 2**20
VMEM_CAP_BYTES = V7X_VMEM_BYTES - 6 * 2**20
CONV_HALO = 16
ATTN_LOOKAHEAD = 2

F32 = jnp.float32
BF16 = jnp.bfloat16
LOG2E = math.log2(math.e)
QK_SCALE_LOG2E = (HEAD_DIM ** -0.5) * LOG2E


def _tile(n, pref, mult):
    if n <= pref:
        return n
    t = (pref // mult) * mult
    while t >= mult:
        if n % t == 0:
            return t
        t -= mult
    raise ValueError(f"no tile for {n} (pref {pref}, mult {mult})")


def _params(n_axes, est_bytes):
    limit = int(min(max(est_bytes * 1.25 + 4 * 2**20, 16 * 2**20), VMEM_CAP_BYTES))
    return pltpu.CompilerParams(dimension_semantics=("arbitrary",) * n_axes, vmem_limit_bytes=limit)


def _nbytes(shape, dtype):
    return math.prod(shape) * jnp.dtype(dtype).itemsize


def _ada_kernel(a_ref, w_ref, b_ref, o_ref, act_ref):
    nv = a_ref.shape[0]
    tn = w_ref.shape[1]

    @pl.when((pl.program_id(0) == 0) & (pl.program_id(1) == 0))
    def _():
        a = a_ref[...]
        act_ref[...] = a * jax.nn.sigmoid(a)

    rows = []
    for v in range(nv):
        parts = []
        for t in range(tn // LANES):
            prod = w_ref[:, t * LANES:(t + 1) * LANES] * act_ref[v]
            parts.append(jnp.sum(prod, axis=0, keepdims=True))
        rows.append(jnp.concatenate(parts, axis=1) + b_ref[...])
    rows.append(jnp.zeros((SUBLANES - nv, tn), F32))
    o_ref[...] = jnp.concatenate(rows, axis=0)


def _ada_mod(cvecs, w_ada, b_ada):
    nv, d = cvecs.shape
    depth, _, n = w_ada.shape
    assert nv < SUBLANES
    tn = _tile(n, 512, LANES)
    a_rep = jnp.broadcast_to(cvecs[:, :, None], (nv, d, LANES))
    est = 2 * (_nbytes((d, tn), F32) + _nbytes((nv, d, LANES), F32)) + _nbytes((nv, d, LANES), F32)
    return pl.pallas_call(
        _ada_kernel,
        out_shape=jax.ShapeDtypeStruct((depth, SUBLANES, n), F32),
        grid=(depth, n // tn),
        in_specs=[
            pl.BlockSpec((nv, d, LANES), lambda l, j: (0, 0, 0)),
            pl.BlockSpec((None, d, tn), lambda l, j: (l, 0, j)),
            pl.BlockSpec((None, 1, tn), lambda l, j: (l, 0, j)),
        ],
        out_specs=pl.BlockSpec((None, SUBLANES, tn), lambda l, j: (l, 0, j)),
        scratch_shapes=[pltpu.VMEM((nv, d, LANES), F32)],
        compiler_params=_params(2, est),
        name="ada_mod",
    )(a_rep, w_ada, b_ada.reshape(depth, 1, n))


def _norm_mod_kernel(x_ref, g_ref, scale_ref, shift_ref, o_ref):
    x = x_ref[...]
    ms = jnp.mean(x * x, axis=-1, keepdims=True)
    y = x * lax.rsqrt(ms + EPS) * g_ref[...]
    o_ref[...] = (y * (1.0 + scale_ref[...]) + shift_ref[...]).astype(o_ref.dtype)


def _norm_mod(x, g, scale, shift):
    m, d = x.shape
    tm = _tile(m, 256, SUBLANES)
    vec = pl.BlockSpec((1, d), lambda i: (0, 0))
    est = 2 * (_nbytes((tm, d), F32) + _nbytes((tm, d), BF16)) + 2 * _nbytes((tm, d), F32)
    return pl.pallas_call(
        _norm_mod_kernel,
        out_shape=jax.ShapeDtypeStruct((m, d), BF16),
        grid=(m // tm,),
        in_specs=[pl.BlockSpec((tm, d), lambda i: (i, 0)), vec, vec, vec],
        out_specs=pl.BlockSpec((tm, d), lambda i: (i, 0)),
        compiler_params=_params(1, est),
        name="norm_mod",
    )(x, g.reshape(1, d), scale, shift)


def _dot(a, b):
    return jnp.dot(a, b, preferred_element_type=F32)


def _dot_nt(a, b):
    return lax.dot_general(a, b, (((1,), (1,)), ((), ())), preferred_element_type=F32)


def _rope_rotate(y, cos, sin_up, sin_dn):
    return y * cos + pltpu.roll(y, 96, 1) * sin_up + pltpu.roll(y, 32, 1) * sin_dn


def _mm_kernel(*refs, n_a, n_w, nk, epilogue, rope):
    a_refs = refs[:n_a]
    w_refs = refs[n_a:n_a + n_w]
    rest = refs[n_a + n_w:]

    if epilogue == "glu":
        val = _dot(a_refs[0][...], w_refs[0][...])
        gate = _dot(a_refs[0][...], w_refs[1][...])
        o_ref = rest[0]
        o_ref[...] = val * jax.nn.sigmoid(gate)
        return

    acc = _dot(a_refs[0][...], w_refs[0][...])
    for p in range(1, n_a):
        acc = acc + _dot(a_refs[p][...], w_refs[p][...])

    if epilogue == "plain":
        rest[0][...] = acc.astype(rest[0].dtype)
    elif epilogue == "plain_t":
        rest[0][...] = acc.T.astype(rest[0].dtype)
    elif epilogue == "sqrelu":
        r = jnp.maximum(acc, 0.0)
        rest[0][...] = (r * r).astype(rest[0].dtype)
    elif epilogue == "qk":
        if rope:
            g_ref, cos_ref, sup_ref, sdn_ref, o_ref = rest
        else:
            g_ref, o_ref = rest
        tn = acc.shape[1]
        for hh in range(tn // HEAD_DIM):
            sl = slice(hh * HEAD_DIM, (hh + 1) * HEAD_DIM)
            blk = acc[:, sl]
            ms = jnp.mean(blk * blk, axis=-1, keepdims=True)
            y = blk * lax.rsqrt(ms + EPS) * g_ref[:, sl]
            if rope:
                y = _rope_rotate(y, cos_ref[...], sup_ref[...], sdn_ref[...])
            o_ref[:, sl] = y.astype(o_ref.dtype)
    elif epilogue == "resid":
        if nk == 1:
            x_ref, gate_ref, o_ref = rest
            o_ref[...] = x_ref[...] + gate_ref[...] * acc
        else:
            x_ref, gate_ref, o_ref, acc_ref = rest
            k = pl.program_id(2)

            @pl.when(k == 0)
            def _():
                acc_ref[...] = acc

            @pl.when((k > 0) & (k < nk - 1))
            def _():
                acc_ref[...] += acc

            @pl.when(k == nk - 1)
            def _():
                o_ref[...] = x_ref[...] + gate_ref[...] * (acc_ref[...] + acc)
    else:
        raise ValueError(epilogue)


def _matmul(a_list, w_list, w_offsets, n_out, epilogue, extras=(), rope=False, out_dtype=BF16,
            tm_pref=1024, tn_pref=1024, tk_pref=4096, name="mm"):
    m, kdim = a_list[0].shape
    n_a, n_w = len(a_list), len(w_list)
    tm = _tile(m, tm_pref, SUBLANES)
    tn = _tile(n_out, tn_pref, LANES)
    tk = _tile(kdim, tk_pref, LANES)
    nk = kdim // tk
    assert nk == 1 or epilogue == "resid"
    grid = (m // tm, n_out // tn, nk)

    in_specs, est = [], 0
    for _ in a_list:
        in_specs.append(pl.BlockSpec((tm, tk), lambda i, j, k: (i, k)))
        est += 2 * _nbytes((tm, tk), BF16)
    for (ro, co) in w_offsets:
        assert ro % tk == 0 and co % tn == 0
        in_specs.append(pl.BlockSpec((tk, tn), functools.partial(
            lambda i, j, k, rb, cb: (k + rb, j + cb), rb=ro // tk, cb=co // tn)))
        est += 2 * _nbytes((tk, tn), BF16)
    operands = list(a_list) + list(w_list)

    col_vec = pl.BlockSpec((1, tn), lambda i, j, k: (0, j))
    row_tab = pl.BlockSpec((tm, HEAD_DIM), lambda i, j, k: (i, 0))
    tile = pl.BlockSpec((tm, tn), lambda i, j, k: (i, j))
    scratch = []
    if epilogue == "qk":
        in_specs += [col_vec] + ([row_tab] * 3 if rope else [])
        est += 6 * _nbytes((tm, HEAD_DIM), F32)
    elif epilogue == "resid":
        in_specs += [tile, col_vec]
        est += 2 * _nbytes((tm, tn), F32)
        if nk > 1:
            scratch.append(pltpu.VMEM((tm, tn), F32))
            est += _nbytes((tm, tn), F32)
    operands += list(extras)
    est += 2 * _nbytes((tm, tn), out_dtype) + 3 * _nbytes((tm, tn), F32)

    out_shape, out_spec = (m, n_out), tile
    if epilogue == "plain_t":
        out_shape, out_spec = (n_out, m), pl.BlockSpec((tn, tm), lambda i, j, k: (j, i))
    return pl.pallas_call(
        functools.partial(_mm_kernel, n_a=n_a, n_w=n_w, nk=nk, epilogue=epilogue, rope=rope),
        out_shape=jax.ShapeDtypeStruct(out_shape, out_dtype),
        grid=grid,
        in_specs=in_specs,
        out_specs=out_spec,
        scratch_shapes=scratch,
        compiler_params=_params(3, est),
        name=name,
    )(*operands)


def _conv_kernel(prev_ref, cur_ref, next_ref, w_ref, cb_ref, g_ref, b_ref, o_ref, ext_ref, y_ref,
                 *, width, rb, cw):
    tm, ch = cur_ref.shape
    half = (width - 1) // 2
    i = pl.program_id(0)
    n = pl.num_programs(0)
    ext_ref[0:CONV_HALO, :] = jnp.where(i > 0, prev_ref[...], 0.0)
    ext_ref[CONV_HALO:CONV_HALO + tm, :] = cur_ref[...]
    ext_ref[CONV_HALO + tm:, :] = jnp.where(i < n - 1, next_ref[...], 0.0)

    def col_body(c, carry):
        c0 = pl.multiple_of(c * cw, cw)
        cols = pl.ds(c0, cw)
        taps = [w_ref[t:t + 1, cols] for t in range(width)]
        bias = cb_ref[:, cols]
        for r in range(tm // rb):
            acc = jnp.zeros((rb, cw), F32) + bias
            for t in range(width):
                acc = acc + ext_ref[pl.ds(r * rb + CONV_HALO - half + t, rb), cols] * taps[t]
            y_ref[r * rb:(r + 1) * rb, cols] = acc
        return carry

    lax.fori_loop(0, ch // cw, col_body, 0)

    y = y_ref[...]
    mu = jnp.mean(y, axis=-1, keepdims=True)
    yc = y - mu
    var = jnp.mean(yc * yc, axis=-1, keepdims=True)
    z = yc * lax.rsqrt(var + EPS) * g_ref[...] + b_ref[...]
    o_ref[...] = (z * jax.nn.sigmoid(z)).astype(o_ref.dtype)


def _conv_branch(u, conv_w, conv_b, ln_g, ln_b):
    m, ch = u.shape
    width = conv_w.shape[0]
    assert (width - 1) // 2 <= CONV_HALO and m % CONV_HALO == 0
    tm = _tile(m, 256, CONV_HALO)
    hb = tm // CONV_HALO
    last = m // CONV_HALO - 1
    vec = pl.BlockSpec((1, ch), lambda i: (0, 0))
    est = (2 * (_nbytes((tm + 2 * CONV_HALO, ch), F32) + _nbytes((tm, ch), BF16))
           + _nbytes((tm + 2 * CONV_HALO, ch), F32) + 4 * _nbytes((tm, ch), F32))
    return pl.pallas_call(
        functools.partial(_conv_kernel, width=width, rb=_tile(tm, 64, SUBLANES), cw=LANES),
        out_shape=jax.ShapeDtypeStruct((m, ch), BF16),
        grid=(m // tm,),
        in_specs=[
            pl.BlockSpec((CONV_HALO, ch), lambda i: (jnp.maximum(i * hb - 1, 0), 0)),
            pl.BlockSpec((tm, ch), lambda i: (i, 0)),
            pl.BlockSpec((CONV_HALO, ch), lambda i: (jnp.minimum((i + 1) * hb, last), 0)),
            pl.BlockSpec((width, ch), lambda i: (0, 0)),
            vec, vec, vec,
        ],
        out_specs=pl.BlockSpec((tm, ch), lambda i: (i, 0)),
        scratch_shapes=[pltpu.VMEM((tm + 2 * CONV_HALO, ch), F32), pltpu.VMEM((tm, ch), F32)],
        compiler_params=_params(1, est),
        name="conv_branch",
    )(u, u, u, conv_w, conv_b.reshape(1, ch), ln_g.reshape(1, ch), ln_b.reshape(1, ch))


def _attn_kernel(*refs, n_src, chunks, n_chain, k_shared, dv, mode, lam_init):
    q_ref = refs[0]
    kv_refs = refs[1:1 + 2 * n_src]
    rest = refs[1 + 2 * n_src:]
    o_ref, acc_ref, sbuf_ref = rest[-3], rest[-2], rest[-1]
    d = HEAD_DIM
    tq = q_ref.shape[0]
    la = ATTN_LOOKAHEAD
    qs = [q_ref[:, g * d:(g + 1) * d] for g in range(n_chain)]
    acc_ref[...] = jnp.zeros_like(acc_ref)
    state = (jnp.full((1, tq), -jnp.inf, F32), jnp.zeros((1, tq), F32)) * n_chain

    for s_idx in range(n_src):
        k_ref, vt_ref = kv_refs[2 * s_idx], kv_refs[2 * s_idx + 1]
        tk = chunks[s_idx]
        n_chunks = k_ref.shape[0] // tk
        unroll = 2 if n_chunks % 2 == 0 else 1
        n_iter = n_chunks // unroll
        stages = [(u, g) for u in range(unroll) for g in range(n_chain)]
        n_st = len(stages)
        assert n_st >= la

        def rows_of(it, u, tk=tk, unroll=unroll):
            start = (it * unroll + u) * tk
            return pl.ds(start if isinstance(start, int) else pl.multiple_of(start, tk), tk)

        def scores(it, u, g, k_ref=k_ref):
            cols = slice(0, d) if k_shared else slice(g * d, (g + 1) * d)
            return _dot_nt(k_ref[rows_of(it, u), cols], qs[g])

        def body(ci, st, vt_ref=vt_ref, tk=tk, n_iter=n_iter, stages=stages, n_st=n_st,
                 rows_of=rows_of, scores=scores):
            st = list(st)
            looped = n_iter > 1
            nxt = jnp.minimum(ci + 1, n_iter - 1) if looped else None
            if looped:
                pending = [sbuf_ref[j, 0:tk, :] for j in range(la)]
            else:
                pending = [scores(ci, *stages[j]) for j in range(la)]
            for t, (u, g) in enumerate(stages):
                if t + la < n_st:
                    pending.append(scores(ci, *stages[t + la]))
                elif looped:
                    j = t + la - n_st
                    sbuf_ref[j, 0:tk, :] = scores(nxt, *stages[j])
                s = pending.pop(0)
                m, l = st[2 * g], st[2 * g + 1]
                m_new = jnp.maximum(m, jnp.max(s, axis=0, keepdims=True))
                alpha = jnp.exp2(m - m_new)
                p = jnp.exp2(s - m_new)
                st[2 * g], st[2 * g + 1] = m_new, alpha * l + jnp.sum(p, axis=0, keepdims=True)
                acc_ref[g] = alpha * acc_ref[g] + _dot(vt_ref[:, rows_of(ci, u)], p.astype(vt_ref.dtype))
            return tuple(st)

        if n_iter > 1:
            for j in range(la):
                sbuf_ref[j, 0:tk, :] = scores(0, *stages[j])
            state = lax.fori_loop(0, n_iter, body, state)
        else:
            state = body(0, state)

    if mode == "gqa":
        for g in range(n_chain):
            o = (acc_ref[g] / state[2 * g + 1]).T
            o_ref[:, g * dv:(g + 1) * dv] = o.astype(o_ref.dtype)
    else:
        lamv_ref, subg_ref = rest[0], rest[1]
        lv = lamv_ref[...]
        lam = (jnp.exp(jnp.sum(lv[0:1] * lv[1:2], axis=-1, keepdims=True))
               - jnp.exp(jnp.sum(lv[2:3] * lv[3:4], axis=-1, keepdims=True)) + lam_init)
        o = (acc_ref[0] / state[1] - lam * (acc_ref[1] / state[3])).T
        ms = jnp.mean(o * o, axis=-1, keepdims=True)
        o = o * lax.rsqrt(ms + EPS) * subg_ref[...] * (1.0 - lam_init)
        o_ref[...] = o.astype(o_ref.dtype)


def _attention(q_arr, q_col0, sources, n_heads, n_chain, k_shared, dv, mode, extras=(), lam_init=0.0,
               tq_pref=256, tk_pref=1024):
    lq = q_arr.shape[0]
    d = HEAD_DIM
    qw = n_chain * d
    kw = d if k_shared else qw
    ow = n_chain * dv if mode == "gqa" else dv
    tq = _tile(lq, tq_pref, LANES)
    in_specs = [pl.BlockSpec((tq, qw), functools.partial(lambda h, i, b: (i, h + b), b=q_col0 // qw))]
    operands, chunks = [q_arr], []
    est = 4 * _nbytes((tq, qw), BF16) + 2 * _nbytes((tq, ow), BF16) + _nbytes((n_chain, dv, tq), F32)
    for (k_arr, k_col0, vt_arr) in sources:
        lk = k_arr.shape[0]
        in_specs.append(pl.BlockSpec((lk, kw), functools.partial(lambda h, i, b: (0, h + b), b=k_col0 // kw)))
        in_specs.append(pl.BlockSpec((dv, lk), lambda h, i: (h, 0)))
        operands += [k_arr, vt_arr]
        tk = _tile(lk, tk_pref, LANES)
        chunks.append(tk)
        est += 2 * (_nbytes((lk, kw), BF16) + _nbytes((dv, lk), BF16)) + 4 * n_chain * _nbytes((tk, tq), F32)
    for e in extras:
        in_specs.append(pl.BlockSpec(e.shape, lambda h, i: (0, 0)))
    operands += list(extras)
    return pl.pallas_call(
        functools.partial(_attn_kernel, n_src=len(sources), chunks=tuple(chunks), n_chain=n_chain,
                          k_shared=k_shared, dv=dv, mode=mode, lam_init=lam_init),
        out_shape=jax.ShapeDtypeStruct((lq, n_heads * ow), BF16),
        grid=(n_heads, lq // tq),
        in_specs=in_specs,
        out_specs=pl.BlockSpec((tq, ow), lambda h, i: (i, h)),
        scratch_shapes=[pltpu.VMEM((n_chain, dv, tq), F32),
                        pltpu.VMEM((ATTN_LOOKAHEAD, max(chunks), tq), F32)],
        compiler_params=_params(2, est + _nbytes((ATTN_LOOKAHEAD, max(chunks), tq), F32)),
        name=mode + "_attn",
    )(*operands)


def _rope_tables(rows):
    n_freq = HEAD_DIM // 4
    tok = jnp.arange(rows * GRID_W, dtype=jnp.int32)
    row = (tok // GRID_W).astype(F32)
    col = (tok % GRID_W).astype(F32)
    inv = ROPE_THETA ** (-jnp.arange(n_freq, dtype=F32) / n_freq)
    cr, sr = jnp.cos(row[:, None] * inv), jnp.sin(row[:, None] * inv)
    cc, sc = jnp.cos(col[:, None] * inv), jnp.sin(col[:, None] * inv)
    z = jnp.zeros_like(sr)
    cos = jnp.concatenate([cr, cr, cc, cc], axis=-1)
    sin_up = jnp.concatenate([-sr, z, -sc, z], axis=-1)
    sin_dn = jnp.concatenate([z, sr, z, sc], axis=-1)
    return cos, sin_up, sin_dn


def _ffn(x, mods, g, w1, w2):
    sf, cf, gf = mods
    h = _norm_mod(x, g, cf, sf)
    hid = _matmul([h], [w1], [(0, 0)], w1.shape[1], "sqrelu", name="ffn_up")
    return _matmul([hid], [w2], [(0, 0)], w2.shape[1], "resid", extras=(x, gf), out_dtype=F32,
                   tk_pref=2048, name="ffn_down")


def _even_mixer(h_lat, h_ctx, x_lat, x_ctx, gm_lat, gm_ctx, tables, w_in, conv_w, conv_b, ln_g, ln_b,
                q_g, k_g, lamv, subg, w_out, layer_idx, need_ctx):
    d_model = h_lat.shape[1]
    conv_ch = conv_w.shape[1]
    diff_w = d_model - conv_ch
    n_heads = diff_w // (2 * HEAD_DIM)
    lam_init = 0.8 - 0.6 * math.exp(-0.3 * layer_idx)
    q_off = 2 * conv_ch
    g_vec = jnp.concatenate([jnp.tile(q_g * QK_SCALE_LOG2E, diff_w // HEAD_DIM),
                             jnp.tile(k_g, diff_w // HEAD_DIM)]).reshape(1, -1)

    def project(h, rope_tabs):
        u = _matmul([h], [w_in, w_in], [(0, 0), (0, conv_ch)], conv_ch, "glu", out_dtype=F32,
                    tn_pref=512, name="in_glu")
        qk = _matmul([h], [w_in], [(0, q_off)], 2 * diff_w, "qk", extras=(g_vec,) + tuple(rope_tabs),
                     rope=bool(rope_tabs), name="in_qk")
        vt = _matmul([h], [w_in], [(0, q_off + 2 * diff_w)], diff_w, "plain_t", name="in_v")
        return u, qk, vt

    def finish(u, o, x, gm):
        a = _conv_branch(u, conv_w, conv_b, ln_g, ln_b)
        return _matmul([a, o], [w_out, w_out], [(0, 0), (conv_ch, 0)], d_model, "resid",
                       extras=(x, gm), out_dtype=F32, name="out_proj")

    def attend(q_arr, sources):
        return _attention(q_arr, 0, sources, n_heads, 2, False, 2 * HEAD_DIM, "diff",
                          extras=(lamv, subg.reshape(1, -1)), lam_init=lam_init)

    u_lat, qk_lat, vt_lat = project(h_lat, tables)
    u_ctx, qk_ctx, vt_ctx = project(h_ctx, ())
    o_lat = attend(qk_lat, [(qk_lat, diff_w, vt_lat), (qk_ctx, diff_w, vt_ctx)])
    x_lat = finish(u_lat, o_lat, x_lat, gm_lat)
    if need_ctx:
        o_ctx = attend(qk_ctx, [(qk_ctx, diff_w, vt_ctx)])
        x_ctx = finish(u_ctx, o_ctx, x_ctx, gm_ctx)
    return x_lat, x_ctx


def _odd_mixer(h_lat, h_ctx, x_lat, x_ctx, gm_lat, gm_ctx, tables, w_in, q_g, k_g, w_out, need_ctx):
    d_model = h_lat.shape[1]
    n_q = d_model // HEAD_DIM
    n_kv = (w_in.shape[1] // HEAD_DIM - n_q) // 2
    group = n_q // n_kv
    q_w, kv_w = n_q * HEAD_DIM, n_kv * HEAD_DIM
    g_vec = jnp.concatenate([jnp.tile(q_g * QK_SCALE_LOG2E, n_q), jnp.tile(k_g, n_kv)]).reshape(1, -1)

    def project(h, rope_tabs, with_q):
        c0 = 0 if with_q else q_w
        qk = _matmul([h], [w_in], [(0, c0)], q_w + kv_w - c0, "qk", extras=(g_vec[:, c0:],) + tuple(rope_tabs),
                     rope=bool(rope_tabs), name="in_qk")
        vt = _matmul([h], [w_in], [(0, q_w + kv_w)], kv_w, "plain_t", name="in_v")
        return qk, vt

    def attend(q_arr, sources):
        return _attention(q_arr, 0, sources, n_kv, group, True, HEAD_DIM, "gqa")

    qk_lat, vt_lat = project(h_lat, tables, True)
    qk_ctx, vt_ctx = project(h_ctx, (), need_ctx)
    k_ctx_col = q_w if need_ctx else 0
    o_lat = attend(qk_lat, [(qk_lat, q_w, vt_lat), (qk_ctx, k_ctx_col, vt_ctx)])
    x_lat = _matmul([o_lat], [w_out], [(0, 0)], d_model, "resid", extras=(x_lat, gm_lat), out_dtype=F32,
                    name="out_proj")
    if need_ctx:
        o_ctx = attend(qk_ctx, [(qk_ctx, q_w, vt_ctx)])
        x_ctx = _matmul([o_ctx], [w_out], [(0, 0)], d_model, "resid", extras=(x_ctx, gm_ctx), out_dtype=F32,
                        name="out_proj")
    return x_lat, x_ctx


def kernel(x, c, ctx, c_ctx, w_ada, b_ada, norm_mix_g, norm_ffn_g, even_w_in, even_conv_w, even_conv_b,
           even_ln_g, even_ln_b, even_q_norm_g, even_k_norm_g, even_lambda_q1, even_lambda_k1,
           even_lambda_q2, even_lambda_k2, even_subln_g, even_w_out, odd_w_in, odd_q_norm_g, odd_k_norm_g,
           odd_w_out, w_ffn1, w_ffn2):
    batch, seq, d_model = x.shape
    depth = w_ada.shape[0]
    assert seq % GRID_W == 0
    tables = _rope_tables(seq // GRID_W)

    mods = _ada_mod(jnp.concatenate([c, c_ctx[None, :]], axis=0), w_ada, b_ada)
    mods = mods.reshape(depth, SUBLANES, 6, 1, d_model)

    bf = lambda w: w.astype(BF16)
    even_w_in, even_w_out, odd_w_in, odd_w_out = bf(even_w_in), bf(even_w_out), bf(odd_w_in), bf(odd_w_out)
    w_ffn1, w_ffn2 = bf(w_ffn1), bf(w_ffn2)

    outs = []
    for b in range(batch):
        x_lat, x_ctx = x[b], ctx[b]
        for l in range(depth):
            need_ctx = l < depth - 1
            sm, cm, gm, sf, cf, gf = (mods[l, b, t] for t in range(6))
            sm_c, cm_c, gm_c, sf_c, cf_c, gf_c = (mods[l, batch, t] for t in range(6))
            h_lat = _norm_mod(x_lat, norm_mix_g[l], cm, sm)
            h_ctx = _norm_mod(x_ctx, norm_mix_g[l], cm_c, sm_c)
            if l % 2 == 0:
                e = l // 2
                lamv = jnp.stack([even_lambda_q1[e], even_lambda_k1[e], even_lambda_q2[e], even_lambda_k2[e]])
                x_lat, x_ctx_new = _even_mixer(
                    h_lat, h_ctx, x_lat, x_ctx, gm, gm_c, tables, even_w_in[e], even_conv_w[e], even_conv_b[e],
                    even_ln_g[e], even_ln_b[e], even_q_norm_g[e], even_k_norm_g[e], lamv, even_subln_g[e],
                    even_w_out[e], l, need_ctx)
            else:
                o = l // 2
                x_lat, x_ctx_new = _odd_mixer(h_lat, h_ctx, x_lat, x_ctx, gm, gm_c, tables, odd_w_in[o],
                                              odd_q_norm_g[o], odd_k_norm_g[o], odd_w_out[o], need_ctx)
            x_lat = _ffn(x_lat, (sf, cf, gf), norm_ffn_g[l], w_ffn1[l], w_ffn2[l])
            if need_ctx:
                x_ctx = _ffn(x_ctx_new, (sf_c, cf_c, gf_c), norm_ffn_g[l], w_ffn1[l], w_ffn2[l])
        outs.append(x_lat)
    return jnp.stack(outs, axis=0)
```

```python
import functools
import math

import jax
import jax.numpy as jnp
from jax import lax
from jax.experimental import pallas as pl
from jax.experimental.pallas import tpu as pltpu

HEAD_DIM = 128
GRID_W = 64
ROPE_THETA = 10000.0
EPS = 1e-6

LANES = 128
MXU_COLS = 256
QK_ROW_GROUP = 256
SUBLANES = 8
V7X_VMEM_BYTES = 64 * 2**20
VMEM_CAP_BYTES = V7X_VMEM_BYTES - 6 * 2**20
CONV_HALO = 16
ATTN_LOOKAHEAD = 3

F32 = jnp.float32
BF16 = jnp.bfloat16
LOG2E = math.log2(math.e)
QK_SCALE_LOG2E = (HEAD_DIM ** -0.5) * LOG2E


def _tile(n, pref, mult):
    if n <= pref:
        return n
    t = (pref // mult) * mult
    while t >= mult:
        if n % t == 0:
            return t
        t -= mult
    raise ValueError(f"no tile for {n} (pref {pref}, mult {mult})")


def _params(n_axes, est_bytes):
    limit = int(min(max(est_bytes * 1.25 + 4 * 2**20, 16 * 2**20), VMEM_CAP_BYTES))
    return pltpu.CompilerParams(dimension_semantics=("arbitrary",) * n_axes, vmem_limit_bytes=limit)


def _nbytes(shape, dtype):
    return math.prod(shape) * jnp.dtype(dtype).itemsize


def _ada_kernel(a_ref, w_ref, b_ref, o_ref, act_ref):
    nv = a_ref.shape[0]
    tn = w_ref.shape[1]

    @pl.when((pl.program_id(0) == 0) & (pl.program_id(1) == 0))
    def _():
        a = a_ref[...]
        act_ref[...] = a * jax.nn.sigmoid(a)

    rows = []
    for v in range(nv):
        parts = []
        for t in range(tn // LANES):
            prod = w_ref[:, t * LANES:(t + 1) * LANES] * act_ref[v]
            parts.append(jnp.sum(prod, axis=0, keepdims=True))
        rows.append(jnp.concatenate(parts, axis=1) + b_ref[...])
    rows.append(jnp.zeros((SUBLANES - nv, tn), F32))
    o_ref[...] = jnp.concatenate(rows, axis=0)


def _ada_mod(cvecs, w_ada, b_ada):
    nv, d = cvecs.shape
    depth, _, n = w_ada.shape
    assert nv < SUBLANES
    tn = _tile(n, 512, LANES)
    a_rep = jnp.broadcast_to(cvecs[:, :, None], (nv, d, LANES))
    est = 2 * (_nbytes((d, tn), F32) + _nbytes((nv, d, LANES), F32)) + _nbytes((nv, d, LANES), F32)
    return pl.pallas_call(
        _ada_kernel,
        out_shape=jax.ShapeDtypeStruct((depth, SUBLANES, n), F32),
        grid=(depth, n // tn),
        in_specs=[
            pl.BlockSpec((nv, d, LANES), lambda l, j: (0, 0, 0)),
            pl.BlockSpec((None, d, tn), lambda l, j: (l, 0, j)),
            pl.BlockSpec((None, 1, tn), lambda l, j: (l, 0, j)),
        ],
        out_specs=pl.BlockSpec((None, SUBLANES, tn), lambda l, j: (l, 0, j)),
        scratch_shapes=[pltpu.VMEM((nv, d, LANES), F32)],
        compiler_params=_params(2, est),
        name="ada_mod",
    )(a_rep, w_ada, b_ada.reshape(depth, 1, n))


def _norm_mod_kernel(x_ref, g_ref, scale_ref, shift_ref, o_ref):
    x = x_ref[...]
    ms = jnp.mean(x * x, axis=-1, keepdims=True)
    y = x * lax.rsqrt(ms + EPS) * g_ref[...]
    o_ref[...] = (y * (1.0 + scale_ref[...]) + shift_ref[...]).astype(o_ref.dtype)


def _norm_mod(x, g, scale, shift):
    m, d = x.shape
    tm = _tile(m, 256, SUBLANES)
    vec = pl.BlockSpec((1, d), lambda i: (0, 0))
    est = 2 * (_nbytes((tm, d), F32) + _nbytes((tm, d), BF16)) + 2 * _nbytes((tm, d), F32)
    return pl.pallas_call(
        _norm_mod_kernel,
        out_shape=jax.ShapeDtypeStruct((m, d), BF16),
        grid=(m // tm,),
        in_specs=[pl.BlockSpec((tm, d), lambda i: (i, 0)), vec, vec, vec],
        out_specs=pl.BlockSpec((tm, d), lambda i: (i, 0)),
        compiler_params=_params(1, est),
        name="norm_mod",
    )(x, g.reshape(1, d), scale, shift)


def _dot(a, b):
    return jnp.dot(a, b, preferred_element_type=F32)


def _dot_nt(a, b):
    return lax.dot_general(a, b, (((1,), (1,)), ((), ())), preferred_element_type=F32)


def _rope_rotate(y, cos, sin_up, sin_dn):
    return y * cos + pltpu.roll(y, 96, 1) * sin_up + pltpu.roll(y, 32, 1) * sin_dn


def _mm_kernel(*refs, n_a, n_w, nk, epilogue, rope):
    a_refs = refs[:n_a]
    w_refs = refs[n_a:n_a + n_w]
    rest = refs[n_a + n_w:]

    if epilogue == "glu":
        val = _dot(a_refs[0][...], w_refs[0][...])
        gate = _dot(a_refs[0][...], w_refs[1][...])
        o_ref = rest[0]
        o_ref[...] = val * jax.nn.sigmoid(gate)
        return

    def partial_sum(cols, rows=slice(None)):
        acc = _dot(a_refs[0][rows, :], w_refs[0][:, cols])
        for p in range(1, n_a):
            acc = acc + _dot(a_refs[p][rows, :], w_refs[p][:, cols])
        return acc

    tm, tn = a_refs[0].shape[0], w_refs[0].shape[1]
    groups = [slice(c0, c0 + MXU_COLS) for c0 in range(0, tn, MXU_COLS)] if tn % MXU_COLS == 0 else [slice(0, tn)]

    if epilogue == "qk":
        if rope:
            g_ref, cos_ref, sup_ref, sdn_ref, o_ref = rest
        else:
            g_ref, o_ref = rest
        rg = _tile(tm, QK_ROW_GROUP, SUBLANES)
        for r0 in range(0, tm, rg):
            rows = slice(r0, r0 + rg)
            acc = partial_sum(slice(0, tn), rows)
            for h0 in range(0, tn, HEAD_DIM):
                sl = slice(h0, h0 + HEAD_DIM)
                blk = acc[:, sl]
                ms = jnp.mean(blk * blk, axis=-1, keepdims=True)
                y = blk * lax.rsqrt(ms + EPS) * g_ref[:, sl]
                if rope:
                    y = _rope_rotate(y, cos_ref[rows, :], sup_ref[rows, :], sdn_ref[rows, :])
                o_ref[rows, sl] = y.astype(o_ref.dtype)
        return

    if epilogue == "resid" and nk > 1:
        x_ref, gate_ref, o_ref, acc_ref = rest
        k = pl.program_id(2)

        @pl.when(k == 0)
        def _():
            acc_ref[...] = jnp.zeros_like(acc_ref)

        for cols in groups:
            acc_ref[:, cols] += partial_sum(cols)

        @pl.when(k == nk - 1)
        def _():
            o_ref[...] = x_ref[...] + gate_ref[...] * acc_ref[...]
        return

    acc = partial_sum(slice(0, tn))

    if epilogue == "plain":
        rest[0][...] = acc.astype(rest[0].dtype)
    elif epilogue == "plain_t":
        rest[0][...] = acc.T.astype(rest[0].dtype)
    elif epilogue == "sqrelu":
        r = jnp.maximum(acc, 0.0)
        rest[0][...] = (r * r).astype(rest[0].dtype)
    elif epilogue == "resid":
        x_ref, gate_ref, o_ref = rest
        o_ref[...] = x_ref[...] + gate_ref[...] * acc
    else:
        raise ValueError(epilogue)


def _matmul(a_list, w_list, w_offsets, n_out, epilogue, extras=(), rope=False, out_dtype=BF16,
            tm_pref=1024, tn_pref=1024, tk_pref=4096, name="mm"):
    m, kdim = a_list[0].shape
    n_a, n_w = len(a_list), len(w_list)
    tm = _tile(m, tm_pref, SUBLANES)
    tn = _tile(n_out, tn_pref, LANES)
    tk = _tile(kdim, tk_pref, LANES)
    nk = kdim // tk
    assert nk == 1 or epilogue == "resid"
    grid = (m // tm, n_out // tn, nk)

    in_specs, est = [], 0
    for _ in a_list:
        in_specs.append(pl.BlockSpec((tm, tk), lambda i, j, k: (i, k)))
        est += 2 * _nbytes((tm, tk), BF16)
    for (ro, co) in w_offsets:
        assert ro % tk == 0 and co % tn == 0
        in_specs.append(pl.BlockSpec((tk, tn), functools.partial(
            lambda i, j, k, rb, cb: (k + rb, j + cb), rb=ro // tk, cb=co // tn)))
        est += 2 * _nbytes((tk, tn), BF16)
    operands = list(a_list) + list(w_list)

    col_vec = pl.BlockSpec((1, tn), lambda i, j, k: (0, j))
    row_tab = pl.BlockSpec((tm, HEAD_DIM), lambda i, j, k: (i, 0))
    tile = pl.BlockSpec((tm, tn), lambda i, j, k: (i, j))
    scratch = []
    if epilogue == "qk":
        in_specs += [col_vec] + ([row_tab] * 3 if rope else [])
        est += 6 * _nbytes((tm, HEAD_DIM), F32)
    elif epilogue == "resid":
        in_specs += [tile, col_vec]
        est += 2 * _nbytes((tm, tn), F32)
        if nk > 1:
            scratch.append(pltpu.VMEM((tm, tn), F32))
            est += _nbytes((tm, tn), F32)
    operands += list(extras)
    est += 2 * _nbytes((tm, tn), out_dtype) + 3 * _nbytes((tm, tn), F32)

    out_shape, out_spec = (m, n_out), tile
    if epilogue == "plain_t":
        out_shape, out_spec = (n_out, m), pl.BlockSpec((tn, tm), lambda i, j, k: (j, i))
    return pl.pallas_call(
        functools.partial(_mm_kernel, n_a=n_a, n_w=n_w, nk=nk, epilogue=epilogue, rope=rope),
        out_shape=jax.ShapeDtypeStruct(out_shape, out_dtype),
        grid=grid,
        in_specs=in_specs,
        out_specs=out_spec,
        scratch_shapes=scratch,
        compiler_params=_params(3, est),
        name=name,
    )(*operands)


def _conv_kernel(prev_ref, cur_ref, next_ref, w_ref, cb_ref, g_ref, b_ref, o_ref, ext_ref, y_ref, sh_ref,
                 *, width, rb, cw):
    tm, ch = cur_ref.shape
    half = (width - 1) // 2
    sh_rows = sh_ref.shape[1]
    i = pl.program_id(0)
    n = pl.num_programs(0)
    ext_ref[0:CONV_HALO, :] = jnp.where(i > 0, prev_ref[...], 0.0)
    ext_ref[CONV_HALO:CONV_HALO + tm, :] = cur_ref[...]
    ext_ref[CONV_HALO + tm:, :] = jnp.where(i < n - 1, next_ref[...], 0.0)

    def col_body(c, carry):
        c0 = pl.multiple_of(c * cw, cw)
        cols = pl.ds(c0, cw)
        taps = [w_ref[t:t + 1, cols] for t in range(width)]
        bias = cb_ref[:, cols]
        for rho in range(SUBLANES):
            sh_ref[rho] = ext_ref[pl.ds(rho, sh_rows), cols]
        for r in range(tm // rb):
            acc = jnp.zeros((rb, cw), F32) + bias
            for t in range(width):
                s = CONV_HALO - half + t
                a0 = r * rb + (s // SUBLANES) * SUBLANES
                acc = acc + sh_ref[s % SUBLANES, a0:a0 + rb, :] * taps[t]
            y_ref[r * rb:(r + 1) * rb, cols] = acc
        return carry

    lax.fori_loop(0, ch // cw, col_body, 0)

    y = y_ref[...]
    mu = jnp.mean(y, axis=-1, keepdims=True)
    yc = y - mu
    var = jnp.mean(yc * yc, axis=-1, keepdims=True)
    z = yc * lax.rsqrt(var + EPS) * g_ref[...] + b_ref[...]
    o_ref[...] = (z * jax.nn.sigmoid(z)).astype(o_ref.dtype)


def _conv_branch(u, conv_w, conv_b, ln_g, ln_b):
    m, ch = u.shape
    width = conv_w.shape[0]
    assert (width - 1) // 2 <= CONV_HALO and m % CONV_HALO == 0
    tm = _tile(m, 256, CONV_HALO)
    hb = tm // CONV_HALO
    last = m // CONV_HALO - 1
    vec = pl.BlockSpec((1, ch), lambda i: (0, 0))
    est = (2 * (_nbytes((tm + 2 * CONV_HALO, ch), F32) + _nbytes((tm, ch), BF16))
           + _nbytes((tm + 2 * CONV_HALO, ch), F32) + 4 * _nbytes((tm, ch), F32))
    return pl.pallas_call(
        functools.partial(_conv_kernel, width=width, rb=_tile(tm, 64, SUBLANES), cw=LANES),
        out_shape=jax.ShapeDtypeStruct((m, ch), BF16),
        grid=(m // tm,),
        in_specs=[
            pl.BlockSpec((CONV_HALO, ch), lambda i: (jnp.maximum(i * hb - 1, 0), 0)),
            pl.BlockSpec((tm, ch), lambda i: (i, 0)),
            pl.BlockSpec((CONV_HALO, ch), lambda i: (jnp.minimum((i + 1) * hb, last), 0)),
            pl.BlockSpec((width, ch), lambda i: (0, 0)),
            vec, vec, vec,
        ],
        out_specs=pl.BlockSpec((tm, ch), lambda i: (i, 0)),
        scratch_shapes=[pltpu.VMEM((tm + 2 * CONV_HALO, ch), F32), pltpu.VMEM((tm, ch), F32),
                        pltpu.VMEM((SUBLANES, tm + 2 * CONV_HALO - SUBLANES, LANES), F32)],
        compiler_params=_params(1, est),
        name="conv_branch",
    )(u, u, u, conv_w, conv_b.reshape(1, ch), ln_g.reshape(1, ch), ln_b.reshape(1, ch))


def _attn_kernel(*refs, n_src, chunks, n_chain, k_shared, dv, mode, lam_init):
    q_ref = refs[0]
    kv_refs = refs[1:1 + 2 * n_src]
    rest = refs[1 + 2 * n_src:]
    o_ref, acc_ref, sbuf_ref = rest[-3], rest[-2], rest[-1]
    d = HEAD_DIM
    tq = q_ref.shape[0]
    qs =[q_ref[:, g * d:(g + 1) * d] for g in range(n_chain)]
    acc_ref[...] = jnp.zeros_like(acc_ref)
    state = (jnp.full((1, tq), -jnp.inf, F32), jnp.zeros((1, tq), F32)) * n_chain

    for s_idx in range(n_src):
        k_ref, vt_ref = kv_refs[2 * s_idx], kv_refs[2 * s_idx + 1]
        tk = chunks[s_idx]
        n_chunks = k_ref.shape[0] // tk
        unroll = 2 if n_chunks % 2 == 0 else 1
        n_iter = n_chunks // unroll
        stages = [(u, g) for u in range(unroll) for g in range(n_chain)]
        n_st = len(stages)
        la = min(ATTN_LOOKAHEAD, n_st - 1 if n_iter > 1 else n_st)

        def rows_of(it, u, tk=tk, unroll=unroll):
            start = (it * unroll + u) * tk
            return pl.ds(start if isinstance(start, int) else pl.multiple_of(start, tk), tk)

        def scores(it, u, g, k_ref=k_ref):
            cols = slice(0, d) if k_shared else slice(g * d, (g + 1) * d)
            return _dot_nt(k_ref[rows_of(it, u), cols], qs[g])

        def body(ci, st, vt_ref=vt_ref, tk=tk, n_iter=n_iter, stages=stages, n_st=n_st, la=la,
                 rows_of=rows_of, scores=scores):
            st = list(st)
            looped = n_iter > 1
            nxt = jnp.minimum(ci + 1, n_iter - 1) if looped else None
            pending = [None] * la if looped else [scores(ci, *stages[j]) for j in range(la)]
            for t, (u, g) in enumerate(stages):
                if t + la < n_st:
                    pending.append(scores(ci, *stages[t + la]))
                elif looped:
                    j = t + la - n_st
                    sbuf_ref[j, 0:tk, :] = scores(nxt, *stages[j])
                s = pending.pop(0)
                if s is None:
                    s = sbuf_ref[t, 0:tk, :]
                m, l = st[2 * g], st[2 * g + 1]
                m_new = jnp.maximum(m, jnp.max(s, axis=0, keepdims=True))
                alpha = jnp.exp2(m - m_new)
                p = jnp.exp2(s - m_new)
                st[2 * g], st[2 * g + 1] = m_new, alpha * l + jnp.sum(p, axis=0, keepdims=True)
                acc_ref[g] = alpha * acc_ref[g] + _dot(vt_ref[:, rows_of(ci, u)], p.astype(vt_ref.dtype))
            return tuple(st)

        if n_iter > 1:
            for j in range(la):
                sbuf_ref[j, 0:tk, :] = scores(0, *stages[j])
            state = lax.fori_loop(0, n_iter, body, state)
        else:
            state = body(0, state)

    if mode == "gqa":
        for g in range(n_chain):
            o = (acc_ref[g] / state[2 * g + 1]).T
            o_ref[:, g * dv:(g + 1) * dv] = o.astype(o_ref.dtype)
    else:
        lamv_ref, subg_ref = rest[0], rest[1]
        lv = lamv_ref[...]
        lam = (jnp.exp(jnp.sum(lv[0:1] * lv[1:2], axis=-1, keepdims=True))
               - jnp.exp(jnp.sum(lv[2:3] * lv[3:4], axis=-1, keepdims=True)) + lam_init)
        o = (acc_ref[0] / state[1] - lam * (acc_ref[1] / state[3])).T
        ms = jnp.mean(o * o, axis=-1, keepdims=True)
        o = o * lax.rsqrt(ms + EPS) * subg_ref[...] * (1.0 - lam_init)
        o_ref[...] = o.astype(o_ref.dtype)


def _attention(q_arr, q_col0, sources, n_heads, n_chain, k_shared, dv, mode, extras=(), lam_init=0.0,
               tq_pref=512, tk_pref=1024):
    lq = q_arr.shape[0]
    d = HEAD_DIM
    qw = n_chain * d
    kw = d if k_shared else qw
    ow = n_chain * dv if mode == "gqa" else dv
    tq = _tile(lq, tq_pref, LANES)
    in_specs = [pl.BlockSpec((tq, qw), functools.partial(lambda h, i, b: (i, h + b), b=q_col0 // qw))]
    operands, chunks = [q_arr], []
    est = 4 * _nbytes((tq, qw), BF16) + 2 * _nbytes((tq, ow), BF16) + _nbytes((n_chain, dv, tq), F32)
    for (k_arr, k_col0, vt_arr) in sources:
        lk = k_arr.shape[0]
        in_specs.append(pl.BlockSpec((lk, kw), functools.partial(lambda h, i, b: (0, h + b), b=k_col0 // kw)))
        in_specs.append(pl.BlockSpec((dv, lk), lambda h, i: (h, 0)))
        operands += [k_arr, vt_arr]
        tk = _tile(lk, tk_pref, LANES)
        chunks.append(tk)
        est += 2 * (_nbytes((lk, kw), BF16) + _nbytes((dv, lk), BF16)) + 4 * n_chain * _nbytes((tk, tq), F32)
    for e in extras:
        in_specs.append(pl.BlockSpec(e.shape, lambda h, i: (0, 0)))
    operands += list(extras)
    return pl.pallas_call(
        functools.partial(_attn_kernel, n_src=len(sources), chunks=tuple(chunks), n_chain=n_chain,
                          k_shared=k_shared, dv=dv, mode=mode, lam_init=lam_init),
        out_shape=jax.ShapeDtypeStruct((lq, n_heads * ow), BF16),
        grid=(n_heads, lq // tq),
        in_specs=in_specs,
        out_specs=pl.BlockSpec((tq, ow), lambda h, i: (i, h)),
        scratch_shapes=[pltpu.VMEM((n_chain, dv, tq), F32),
                        pltpu.VMEM((ATTN_LOOKAHEAD, max(chunks), tq), F32)],
        compiler_params=_params(2, est + _nbytes((ATTN_LOOKAHEAD, max(chunks), tq), F32)),
        name=mode + "_attn",
    )(*operands)


def _rope_tables(rows):
    n_freq = HEAD_DIM // 4
    tok = jnp.arange(rows * GRID_W, dtype=jnp.int32)
    row = (tok // GRID_W).astype(F32)
    col = (tok % GRID_W).astype(F32)
    inv = ROPE_THETA ** (-jnp.arange(n_freq, dtype=F32) / n_freq)
    cr, sr = jnp.cos(row[:, None] * inv), jnp.sin(row[:, None] * inv)
    cc, sc = jnp.cos(col[:, None] * inv), jnp.sin(col[:, None] * inv)
    z = jnp.zeros_like(sr)
    cos = jnp.concatenate([cr, cr, cc, cc], axis=-1)
    sin_up = jnp.concatenate([-sr, z, -sc, z], axis=-1)
    sin_dn = jnp.concatenate([z, sr, z, sc], axis=-1)
    return cos, sin_up, sin_dn


def _ffn(x, mods, g, w1, w2):
    sf, cf, gf = mods
    h = _norm_mod(x, g, cf, sf)
    hid = _matmul([h], [w1], [(0, 0)], w1.shape[1], "sqrelu", name="ffn_up")
    return _matmul([hid], [w2], [(0, 0)], w2.shape[1], "resid", extras=(x, gf), out_dtype=F32,
                   tk_pref=2048, name="ffn_down")


def _even_mixer(h_lat, h_ctx, x_lat, x_ctx, gm_lat, gm_ctx, tables, w_in, conv_w, conv_b, ln_g, ln_b,
                q_g, k_g, lamv, subg, w_out, layer_idx, need_ctx):
    d_model = h_lat.shape[1]
    conv_ch = conv_w.shape[1]
    diff_w = d_model - conv_ch
    n_heads = diff_w // (2 * HEAD_DIM)
    lam_init = 0.8 - 0.6 * math.exp(-0.3 * layer_idx)
    q_off = 2 * conv_ch
    g_vec = jnp.concatenate([jnp.tile(q_g * QK_SCALE_LOG2E, diff_w // HEAD_DIM),
                             jnp.tile(k_g, diff_w // HEAD_DIM)]).reshape(1, -1)

    def project(h, rope_tabs):
        u = _matmul([h], [w_in, w_in], [(0, 0), (0, conv_ch)], conv_ch, "glu", out_dtype=F32,
                    tn_pref=512, name="in_glu")
        qk = _matmul([h], [w_in], [(0, q_off)], 2 * diff_w, "qk", extras=(g_vec,) + tuple(rope_tabs),
                     rope=bool(rope_tabs), name="in_qk")
        vt = _matmul([h], [w_in], [(0, q_off + 2 * diff_w)], diff_w, "plain_t", name="in_v")
        return u, qk, vt

    def finish(u, o, x, gm):
        a = _conv_branch(u, conv_w, conv_b, ln_g, ln_b)
        return _matmul([a, o], [w_out, w_out], [(0, 0), (conv_ch, 0)], d_model, "resid",
                       extras=(x, gm), out_dtype=F32, name="out_proj")

    def attend(q_arr, sources):
        return _attention(q_arr, 0, sources, n_heads, 2, False, 2 * HEAD_DIM, "diff",
                          extras=(lamv, subg.reshape(1, -1)), lam_init=lam_init)

    u_lat, qk_lat, vt_lat = project(h_lat, tables)
    u_ctx, qk_ctx, vt_ctx = project(h_ctx, ())
    o_lat = attend(qk_lat, [(qk_lat, diff_w, vt_lat), (qk_ctx, diff_w, vt_ctx)])
    x_lat = finish(u_lat, o_lat, x_lat, gm_lat)
    if need_ctx:
        o_ctx = attend(qk_ctx, [(qk_ctx, diff_w, vt_ctx)])
        x_ctx = finish(u_ctx, o_ctx, x_ctx, gm_ctx)
    return x_lat, x_ctx


def _odd_mixer(h_lat, h_ctx, x_lat, x_ctx, gm_lat, gm_ctx, tables, w_in, q_g, k_g, w_out, need_ctx):
    d_model = h_lat.shape[1]
    n_q = d_model // HEAD_DIM
    n_kv = (w_in.shape[1] // HEAD_DIM - n_q) // 2
    group = n_q // n_kv
    q_w, kv_w = n_q * HEAD_DIM, n_kv * HEAD_DIM
    g_vec = jnp.concatenate([jnp.tile(q_g * QK_SCALE_LOG2E, n_q), jnp.tile(k_g, n_kv)]).reshape(1, -1)

    def project(h, rope_tabs, with_q):
        c0 = 0 if with_q else q_w
        qk = _matmul([h], [w_in], [(0, c0)], q_w + kv_w - c0, "qk", extras=(g_vec[:, c0:],) + tuple(rope_tabs),
                     rope=bool(rope_tabs), name="in_qk")
        vt = _matmul([h], [w_in], [(0, q_w + kv_w)], kv_w, "plain_t", name="in_v")
        return qk, vt

    def attend(q_arr, sources):
        return _attention(q_arr, 0, sources, n_kv, group, True, HEAD_DIM, "gqa")

    qk_lat, vt_lat = project(h_lat, tables, True)
    qk_ctx, vt_ctx = project(h_ctx, (), need_ctx)
    k_ctx_col = q_w if need_ctx else 0
    o_lat = attend(qk_lat, [(qk_lat, q_w, vt_lat), (qk_ctx, k_ctx_col, vt_ctx)])
    x_lat = _matmul([o_lat], [w_out], [(0, 0)], d_model, "resid", extras=(x_lat, gm_lat), out_dtype=F32,
                    name="out_proj")
    if need_ctx:
        o_ctx = attend(qk_ctx, [(qk_ctx, q_w, vt_ctx)])
        x_ctx = _matmul([o_ctx], [w_out], [(0, 0)], d_model, "resid", extras=(x_ctx, gm_ctx), out_dtype=F32,
                        name="out_proj")
    return x_lat, x_ctx


def kernel(x, c, ctx, c_ctx, w_ada, b_ada, norm_mix_g, norm_ffn_g, even_w_in, even_conv_w, even_conv_b,
           even_ln_g, even_ln_b, even_q_norm_g, even_k_norm_g, even_lambda_q1, even_lambda_k1,
           even_lambda_q2, even_lambda_k2, even_subln_g, even_w_out, odd_w_in, odd_q_norm_g, odd_k_norm_g,
           odd_w_out, w_ffn1, w_ffn2):
    batch, seq, d_model = x.shape
    depth = w_ada.shape[0]
    assert seq % GRID_W == 0
    tables = _rope_tables(seq // GRID_W)

    mods = _ada_mod(jnp.concatenate([c, c_ctx[None, :]], axis=0), w_ada, b_ada)
    mods = mods.reshape(depth, SUBLANES, 6, 1, d_model)

    per_layer = lambda w: [w[i].astype(BF16) for i in range(w.shape[0])]
    even_w_in, even_w_out, odd_w_in, odd_w_out = map(per_layer, (even_w_in, even_w_out, odd_w_in, odd_w_out))
    w_ffn1, w_ffn2 = per_layer(w_ffn1), per_layer(w_ffn2)

    outs = []
    for b in range(batch):
        x_lat, x_ctx = x[b], ctx[b]
        for l in range(depth):
            need_ctx = l < depth - 1
            sm, cm, gm, sf, cf, gf = (mods[l, b, t] for t in range(6))
            sm_c, cm_c, gm_c, sf_c, cf_c, gf_c = (mods[l, batch, t] for t in range(6))
            h_lat = _norm_mod(x_lat, norm_mix_g[l], cm, sm)
            h_ctx = _norm_mod(x_ctx, norm_mix_g[l], cm_c, sm_c)
            if l % 2 == 0:
                e = l // 2
                lamv = jnp.stack([even_lambda_q1[e], even_lambda_k1[e], even_lambda_q2[e], even_lambda_k2[e]])
                x_lat, x_ctx_new = _even_mixer(
                    h_lat, h_ctx, x_lat, x_ctx, gm, gm_c, tables, even_w_in[e], even_conv_w[e], even_conv_b[e],
                    even_ln_g[e], even_ln_b[e], even_q_norm_g[e], even_k_norm_g[e], lamv, even_subln_g[e],
                    even_w_out[e], l, need_ctx)
            else:
                o = l // 2
                x_lat, x_ctx_new = _odd_mixer(h_lat, h_ctx, x_lat, x_ctx, gm, gm_c, tables, odd_w_in[o],
                                              odd_q_norm_g[o], odd_k_norm_g[o], odd_w_out[o], need_ctx)
            x_lat = _ffn(x_lat, (sf, cf, gf), norm_ffn_g[l], w_ffn1[l], w_ffn2[l])
            if need_ctx:
                x_ctx = _ffn(x_ctx_new, (sf_c, cf_c, gf_c), norm_ffn_g[l], w_ffn1[l], w_ffn2[l])
        outs.append(x_lat)
    return jnp.stack(outs, axis=0)
```

```python
import functools
import math
from typing import NamedTuple

import jax
import jax.numpy as jnp
from jax import lax
from jax.experimental import pallas as pl
from jax.experimental.pallas import tpu as pltpu

HEAD_DIM = 128
GRID_W = 64
ROPE_THETA = 10000.0
EPS = 1e-6

LANES = 128
MXU_COLS = 256
QK_ROW_GROUP = 256
SUBLANES = 8
PACKED_ROWS = 16
V7X_VMEM_BYTES = 64 * 2**20
VMEM_CAP_BYTES = V7X_VMEM_BYTES - 6 * 2**20
CONV_HALO = 16
ATTN_LOOKAHEAD = 3

F32 = jnp.float32
BF16 = jnp.bfloat16
ATTN_EXP_DTYPE = F32
LOG2E = math.log2(math.e)
QK_SCALE_LOG2E = (HEAD_DIM ** -0.5) * LOG2E


def _tile(n, pref, mult):
    if n <= pref:
        return n
    t = (pref // mult) * mult
    while t >= mult:
        if n % t == 0:
            return t
        t -= mult
    raise ValueError(f"no tile for {n} (pref {pref}, mult {mult})")


def _params(n_axes, est_bytes):
    limit = int(min(max(est_bytes * 1.25 + 4 * 2**20, 16 * 2**20), VMEM_CAP_BYTES))
    return pltpu.CompilerParams(dimension_semantics=("arbitrary",) * n_axes, vmem_limit_bytes=limit)


def _nbytes(shape, dtype):
    return math.prod(shape) * jnp.dtype(dtype).itemsize


def _ada_kernel(a_ref, w_ref, b_ref, o_ref, act_ref):
    nv = a_ref.shape[0]
    tn = w_ref.shape[1]

    @pl.when((pl.program_id(0) == 0) & (pl.program_id(1) == 0))
    def _():
        a = a_ref[...]
        act_ref[...] = a * jax.nn.sigmoid(a)

    rows = []
    for v in range(nv):
        parts = []
        for t in range(tn // LANES):
            prod = w_ref[:, t * LANES:(t + 1) * LANES] * act_ref[v]
            parts.append(jnp.sum(prod, axis=0, keepdims=True))
        rows.append(jnp.concatenate(parts, axis=1) + b_ref[...])
    rows.append(jnp.zeros((SUBLANES - nv, tn), F32))
    o_ref[...] = jnp.concatenate(rows, axis=0)


def _ada_mod(cvecs, w_ada, b_ada):
    nv, d = cvecs.shape
    depth, _, n = w_ada.shape
    assert nv < SUBLANES
    tn = _tile(n, 512, LANES)
    a_rep = jnp.broadcast_to(cvecs[:, :, None], (nv, d, LANES))
    est = 2 * (_nbytes((d, tn), F32) + _nbytes((nv, d, LANES), F32)) + _nbytes((nv, d, LANES), F32)
    return pl.pallas_call(
        _ada_kernel,
        out_shape=jax.ShapeDtypeStruct((depth, SUBLANES, n), F32),
        grid=(depth, n // tn),
        in_specs=[
            pl.BlockSpec((nv, d, LANES), lambda l, j: (0, 0, 0)),
            pl.BlockSpec((None, d, tn), lambda l, j: (l, 0, j)),
            pl.BlockSpec((None, 1, tn), lambda l, j: (l, 0, j)),
        ],
        out_specs=pl.BlockSpec((None, SUBLANES, tn), lambda l, j: (l, 0, j)),
        scratch_shapes=[pltpu.VMEM((nv, d, LANES), F32)],
        compiler_params=_params(2, est),
        name="ada_mod",
    )(a_rep, w_ada, b_ada.reshape(depth, 1, n))


def _norm_mod_kernel(x_ref, g_ref, scale_ref, shift_ref, o_ref):
    x = x_ref[...]
    ms = jnp.mean(x * x, axis=-1, keepdims=True)
    y = x * lax.rsqrt(ms + EPS) * g_ref[...]
    o_ref[...] = (y * (1.0 + scale_ref[...]) + shift_ref[...]).astype(o_ref.dtype)


def _norm_mod(x, g, scale, shift):
    m, d = x.shape
    tm = _tile(m, 256, SUBLANES)
    vec = pl.BlockSpec((1, d), lambda i: (0, 0))
    est = 2 * (_nbytes((tm, d), F32) + _nbytes((tm, d), BF16)) + 2 * _nbytes((tm, d), F32)
    return pl.pallas_call(
        _norm_mod_kernel,
        out_shape=jax.ShapeDtypeStruct((m, d), BF16),
        grid=(m // tm,),
        in_specs=[pl.BlockSpec((tm, d), lambda i: (i, 0)), vec, vec, vec],
        out_specs=pl.BlockSpec((tm, d), lambda i: (i, 0)),
        compiler_params=_params(1, est),
        name="norm_mod",
    )(x, g.reshape(1, d), scale, shift)


def _dot(a, b):
    return jnp.dot(a, b, preferred_element_type=F32)


def _dot_nt(a, b):
    return lax.dot_general(a, b, (((1,), (1,)), ((), ())), preferred_element_type=F32)


def _rope_rotate(y, cos, sin_up, sin_dn):
    return y * cos + pltpu.roll(y, 96, 1) * sin_up + pltpu.roll(y, 32, 1) * sin_dn


def _mm_kernel(*refs, n_a, n_w, nk, epilogue, rope):
    a_refs = refs[:n_a]
    w_refs = refs[n_a:n_a + n_w]
    rest = refs[n_a + n_w:]

    if epilogue == "glu":
        val = _dot(a_refs[0][...], w_refs[0][...])
        gate = _dot(a_refs[0][...], w_refs[1][...])
        o_ref = rest[0]
        o_ref[...] = val * jax.nn.sigmoid(gate)
        return

    def partial_sum(cols, rows=slice(None)):
        acc = _dot(a_refs[0][rows, :], w_refs[0][:, cols])
        for p in range(1, n_a):
            acc = acc + _dot(a_refs[p][rows, :], w_refs[p][:, cols])
        return acc

    tm, tn = a_refs[0].shape[0], w_refs[0].shape[1]
    groups = [slice(c0, c0 + MXU_COLS) for c0 in range(0, tn, MXU_COLS)] if tn % MXU_COLS == 0 else [slice(0, tn)]

    if epilogue == "qk":
        if rope:
            g_ref, cos_ref, sup_ref, sdn_ref, o_ref = rest
        else:
            g_ref, o_ref = rest
        rg = _tile(tm, QK_ROW_GROUP, SUBLANES)
        for r0 in range(0, tm, rg):
            rows = slice(r0, r0 + rg)
            acc = partial_sum(slice(0, tn), rows)
            for h0 in range(0, tn, HEAD_DIM):
                sl = slice(h0, h0 + HEAD_DIM)
                blk = acc[:, sl]
                ms = jnp.mean(blk * blk, axis=-1, keepdims=True)
                y = blk * lax.rsqrt(ms + EPS) * g_ref[:, sl]
                if rope:
                    y = _rope_rotate(y, cos_ref[rows, :], sup_ref[rows, :], sdn_ref[rows, :])
                o_ref[rows, sl] = y.astype(o_ref.dtype)
        return

    if epilogue == "resid" and nk > 1:
        x_ref, gate_ref, o_ref, acc_ref = rest
        k = pl.program_id(2)

        @pl.when(k == 0)
        def _():
            acc_ref[...] = jnp.zeros_like(acc_ref)

        for cols in groups:
            acc_ref[:, cols] += partial_sum(cols)

        @pl.when(k == nk - 1)
        def _():
            o_ref[...] = x_ref[...] + gate_ref[...] * acc_ref[...]
        return

    acc = partial_sum(slice(0, tn))

    if epilogue == "plain":
        rest[0][...] = acc.astype(rest[0].dtype)
    elif epilogue == "plain_t":
        rest[0][...] = acc.T.astype(rest[0].dtype)
    elif epilogue == "sqrelu":
        r = jnp.maximum(acc, 0.0)
        rest[0][...] = (r * r).astype(rest[0].dtype)
    elif epilogue == "resid":
        x_ref, gate_ref, o_ref = rest
        o_ref[...] = x_ref[...] + gate_ref[...] * acc
    else:
        raise ValueError(epilogue)


class LayerWeight(NamedTuple):
    stack: jax.Array
    layer: int

    @property
    def shape(self):
        return self.stack.shape[1:]


def _matmul(a_list, w_list, w_offsets, n_out, epilogue, extras=(), rope=False, out_dtype=BF16,
            tm_pref=1024, tn_pref=1024, tk_pref=4096, name="mm"):
    m, kdim = a_list[0].shape
    n_a, n_w = len(a_list), len(w_list)
    tm = _tile(m, tm_pref, SUBLANES)
    tn = _tile(n_out, tn_pref, LANES)
    tk = _tile(kdim, tk_pref, LANES)
    nk = kdim // tk
    assert nk == 1 or epilogue == "resid"
    grid = (m // tm, n_out // tn, nk)

    in_specs, est = [], 0
    for _ in a_list:
        in_specs.append(pl.BlockSpec((tm, tk), lambda i, j, k: (i, k)))
        est += 2 * _nbytes((tm, tk), BF16)
    for w, (ro, co) in zip(w_list, w_offsets):
        assert ro % tk == 0 and co % tn == 0
        in_specs.append(pl.BlockSpec((None, tk, tn), functools.partial(
            lambda i, j, k, layer, rb, cb: (layer, k + rb, j + cb), layer=w.layer, rb=ro // tk, cb=co // tn)))
        est += 2 * _nbytes((tk, tn), BF16)
    operands = list(a_list) + [w.stack for w in w_list]

    col_vec = pl.BlockSpec((1, tn), lambda i, j, k: (0, j))
    row_tab = pl.BlockSpec((tm, HEAD_DIM), lambda i, j, k: (i, 0))
    tile = pl.BlockSpec((tm, tn), lambda i, j, k: (i, j))
    scratch = []
    if epilogue == "qk":
        in_specs += [col_vec] + ([row_tab] * 3 if rope else [])
        est += 6 * _nbytes((tm, HEAD_DIM), F32)
    elif epilogue == "resid":
        in_specs += [tile, col_vec]
        est += 2 * _nbytes((tm, tn), F32)
        if nk > 1:
            scratch.append(pltpu.VMEM((tm, tn), F32))
            est += _nbytes((tm, tn), F32)
    operands += list(extras)
    est += 2 * _nbytes((tm, tn), out_dtype) + 3 * _nbytes((tm, tn), F32)

    out_shape, out_spec = (m, n_out), tile
    if epilogue == "plain_t":
        out_shape, out_spec = (n_out, m), pl.BlockSpec((tn, tm), lambda i, j, k: (j, i))
    return pl.pallas_call(
        functools.partial(_mm_kernel, n_a=n_a, n_w=n_w, nk=nk, epilogue=epilogue, rope=rope),
        out_shape=jax.ShapeDtypeStruct(out_shape, out_dtype),
        grid=grid,
        in_specs=in_specs,
        out_specs=out_spec,
        scratch_shapes=scratch,
        compiler_params=_params(3, est),
        name=name,
    )(*operands)


def _conv_kernel(prev_ref, cur_ref, next_ref, w_ref, cb_ref, g_ref, b_ref, o_ref, ext_ref, y_ref, sh_ref,
                 *, width, rb, cw):
    tm, ch = cur_ref.shape
    half = (width - 1) // 2
    sh_rows = sh_ref.shape[1]
    i = pl.program_id(0)
    n = pl.num_programs(0)
    ext_ref[0:CONV_HALO, :] = jnp.where(i > 0, prev_ref[...], 0.0)
    ext_ref[CONV_HALO:CONV_HALO + tm, :] = cur_ref[...]
    ext_ref[CONV_HALO + tm:, :] = jnp.where(i < n - 1, next_ref[...], 0.0)

    def col_body(c, carry):
        c0 = pl.multiple_of(c * cw, cw)
        cols = pl.ds(c0, cw)
        taps = [w_ref[t:t + 1, cols] for t in range(width)]
        bias = cb_ref[:, cols]
        for rho in range(SUBLANES):
            sh_ref[rho] = ext_ref[pl.ds(rho, sh_rows), cols]
        for r in range(tm // rb):
            acc = jnp.zeros((rb, cw), F32) + bias
            for t in range(width):
                s = CONV_HALO - half + t
                a0 = r * rb + (s // SUBLANES) * SUBLANES
                acc = acc + sh_ref[s % SUBLANES, a0:a0 + rb, :] * taps[t]
            y_ref[r * rb:(r + 1) * rb, cols] = acc
        return carry

    lax.fori_loop(0, ch // cw, col_body, 0)

    y = y_ref[...]
    mu = jnp.mean(y, axis=-1, keepdims=True)
    yc = y - mu
    var = jnp.mean(yc * yc, axis=-1, keepdims=True)
    z = yc * lax.rsqrt(var + EPS) * g_ref[...] + b_ref[...]
    o_ref[...] = (z * jax.nn.sigmoid(z)).astype(o_ref.dtype)


def _conv_branch(u, conv_w, conv_b, ln_g, ln_b):
    m, ch = u.shape
    width = conv_w.shape[0]
    assert (width - 1) // 2 <= CONV_HALO and m % CONV_HALO == 0
    tm = _tile(m, 256, CONV_HALO)
    hb = tm // CONV_HALO
    last = m // CONV_HALO - 1
    vec = pl.BlockSpec((1, ch), lambda i: (0, 0))
    est = (2 * (_nbytes((tm + 2 * CONV_HALO, ch), F32) + _nbytes((tm, ch), BF16))
           + _nbytes((tm + 2 * CONV_HALO, ch), F32) + 4 * _nbytes((tm, ch), F32))
    return pl.pallas_call(
        functools.partial(_conv_kernel, width=width, rb=_tile(tm, 64, SUBLANES), cw=LANES),
        out_shape=jax.ShapeDtypeStruct((m, ch), BF16),
        grid=(m // tm,),
        in_specs=[
            pl.BlockSpec((CONV_HALO, ch), lambda i: (jnp.maximum(i * hb - 1, 0), 0)),
            pl.BlockSpec((tm, ch), lambda i: (i, 0)),
            pl.BlockSpec((CONV_HALO, ch), lambda i: (jnp.minimum((i + 1) * hb, last), 0)),
            pl.BlockSpec((width, ch), lambda i: (0, 0)),
            vec, vec, vec,
        ],
        out_specs=pl.BlockSpec((tm, ch), lambda i: (i, 0)),
        scratch_shapes=[pltpu.VMEM((tm + 2 * CONV_HALO, ch), F32), pltpu.VMEM((tm, ch), F32),
                        pltpu.VMEM((SUBLANES, tm + 2 * CONV_HALO - SUBLANES, LANES), F32)],
        compiler_params=_params(1, est),
        name="conv_branch",
    )(u, u, u, conv_w, conv_b.reshape(1, ch), ln_g.reshape(1, ch), ln_b.reshape(1, ch))


def _attn_kernel(*refs, n_src, chunks, n_chain, k_shared, dv, mode, lam_init):
    q_ref = refs[0]
    kv_refs = refs[1:1 + 2 * n_src]
    rest = refs[1 + 2 * n_src:]
    o_ref, acc_ref, sbuf_ref = rest[-3], rest[-2], rest[-1]
    d = HEAD_DIM
    tq = q_ref.shape[0]
    qs =[q_ref[:, g * d:(g + 1) * d] for g in range(n_chain)]
    acc_ref[...] = jnp.zeros_like(acc_ref)
    state = (jnp.full((1, tq), -jnp.inf, F32),) * n_chain

    for s_idx in range(n_src):
        k_ref, vt_ref = kv_refs[2 * s_idx], kv_refs[2 * s_idx + 1]
        tk = chunks[s_idx]
        n_chunks = k_ref.shape[0] // tk
        unroll = 2 if n_chunks % 2 == 0 else 1
        n_iter = n_chunks // unroll
        stages = [(u, g) for u in range(unroll) for g in range(n_chain)]
        n_st = len(stages)
        la = min(ATTN_LOOKAHEAD, n_st - 1 if n_iter > 1 else n_st)

        def rows_of(it, u, tk=tk, unroll=unroll):
            start = (it * unroll + u) * tk
            return pl.ds(start if isinstance(start, int) else pl.multiple_of(start, tk), tk)

        def scores(it, u, g, k_ref=k_ref):
            cols = slice(0, d) if k_shared else slice(g * d, (g + 1) * d)
            return _dot_nt(k_ref[rows_of(it, u), cols], qs[g])

        def body(ci, st, vt_ref=vt_ref, tk=tk, n_iter=n_iter, stages=stages, n_st=n_st, la=la,
                 rows_of=rows_of, scores=scores):
            st = list(st)
            looped = n_iter > 1
            nxt = jnp.minimum(ci + 1, n_iter - 1) if looped else None
            pending = [None] * la if looped else [scores(ci, *stages[j]) for j in range(la)]
            for t, (u, g) in enumerate(stages):
                if t + la < n_st:
                    pending.append(scores(ci, *stages[t + la]))
                elif looped:
                    j = t + la - n_st
                    sbuf_ref[j, 0:tk, :] = scores(nxt, *stages[j])
                s = pending.pop(0)
                if s is None:
                    s = sbuf_ref[t, 0:tk, :]
                m = st[g]
                m_new = jnp.maximum(m, jnp.max(s, axis=0, keepdims=True))
                alpha = jnp.exp2(m - m_new)
                p = jnp.exp2((s - m_new).astype(ATTN_EXP_DTYPE)).astype(vt_ref.dtype)
                st[g] = m_new
                vt1 = jnp.concatenate([vt_ref[:, rows_of(ci, u)], jnp.ones((PACKED_ROWS, tk), vt_ref.dtype)], axis=0)
                acc_ref[g] = alpha * acc_ref[g] + _dot(vt1, p)
            return tuple(st)

        if n_iter > 1:
            for j in range(la):
                sbuf_ref[j, 0:tk, :] = scores(0, *stages[j])
            state = lax.fori_loop(0, n_iter, body, state)
        else:
            state = body(0, state)

    if mode == "gqa":
        for g in range(n_chain):
            o = (acc_ref[g, 0:dv, :] / acc_ref[g, dv:dv + 1, :]).T
            o_ref[:, g * dv:(g + 1) * dv] = o.astype(o_ref.dtype)
    else:
        lamv_ref, subg_ref = rest[0], rest[1]
        lv = lamv_ref[...]
        lam = (jnp.exp(jnp.sum(lv[0:1] * lv[1:2], axis=-1, keepdims=True))
               - jnp.exp(jnp.sum(lv[2:3] * lv[3:4], axis=-1, keepdims=True)) + lam_init)
        o = (acc_ref[0, 0:dv, :] / acc_ref[0, dv:dv + 1, :]
             - lam * (acc_ref[1, 0:dv, :] / acc_ref[1, dv:dv + 1, :])).T
        ms = jnp.mean(o * o, axis=-1, keepdims=True)
        o = o * lax.rsqrt(ms + EPS) * subg_ref[...] * (1.0 - lam_init)
        o_ref[...] = o.astype(o_ref.dtype)


def _attention(q_arr, q_col0, sources, n_heads, n_chain, k_shared, dv, mode, extras=(), lam_init=0.0,
               tq_pref=512, tk_pref=1024):
    lq = q_arr.shape[0]
    d = HEAD_DIM
    qw = n_chain * d
    kw = d if k_shared else qw
    ow = n_chain * dv if mode == "gqa" else dv
    tq = _tile(lq, tq_pref, LANES)
    in_specs = [pl.BlockSpec((tq, qw), functools.partial(lambda h, i, b: (i, h + b), b=q_col0 // qw))]
    operands, chunks = [q_arr], []
    est = 4 * _nbytes((tq, qw), BF16) + 2 * _nbytes((tq, ow), BF16) + _nbytes((n_chain, dv + PACKED_ROWS, tq), F32)
    for (k_arr, k_col0, vt_arr) in sources:
        lk = k_arr.shape[0]
        in_specs.append(pl.BlockSpec((lk, kw), functools.partial(lambda h, i, b: (0, h + b), b=k_col0 // kw)))
        in_specs.append(pl.BlockSpec((dv, lk), lambda h, i: (h, 0)))
        operands += [k_arr, vt_arr]
        tk = _tile(lk, tk_pref, LANES)
        chunks.append(tk)
        est += 2 * (_nbytes((lk, kw), BF16) + _nbytes((dv, lk), BF16)) + 4 * n_chain * _nbytes((tk, tq), F32)
    for e in extras:
        in_specs.append(pl.BlockSpec(e.shape, lambda h, i: (0, 0)))
    operands += list(extras)
    return pl.pallas_call(
        functools.partial(_attn_kernel, n_src=len(sources), chunks=tuple(chunks), n_chain=n_chain,
                          k_shared=k_shared, dv=dv, mode=mode, lam_init=lam_init),
        out_shape=jax.ShapeDtypeStruct((lq, n_heads * ow), BF16),
        grid=(n_heads, lq // tq),
        in_specs=in_specs,
        out_specs=pl.BlockSpec((tq, ow), lambda h, i: (i, h)),
        scratch_shapes=[pltpu.VMEM((n_chain, dv + PACKED_ROWS, tq), F32),
                        pltpu.VMEM((ATTN_LOOKAHEAD, max(chunks), tq), F32)],
        compiler_params=_params(2, est + _nbytes((ATTN_LOOKAHEAD, max(chunks), tq), F32)),
        name=mode + "_attn",
    )(*operands)


def _rope_tables(rows):
    n_freq = HEAD_DIM // 4
    tok = jnp.arange(rows * GRID_W, dtype=jnp.int32)
    row = (tok // GRID_W).astype(F32)
    col = (tok % GRID_W).astype(F32)
    inv = ROPE_THETA ** (-jnp.arange(n_freq, dtype=F32) / n_freq)
    cr, sr = jnp.cos(row[:, None] * inv), jnp.sin(row[:, None] * inv)
    cc, sc = jnp.cos(col[:, None] * inv), jnp.sin(col[:, None] * inv)
    z = jnp.zeros_like(sr)
    cos = jnp.concatenate([cr, cr, cc, cc], axis=-1)
    sin_up = jnp.concatenate([-sr, z, -sc, z], axis=-1)
    sin_dn = jnp.concatenate([z, sr, z, sc], axis=-1)
    return cos, sin_up, sin_dn


def _ffn(x, mods, g, w1, w2):
    sf, cf, gf = mods
    h = _norm_mod(x, g, cf, sf)
    hid = _matmul([h], [w1], [(0, 0)], w1.shape[1], "sqrelu", name="ffn_up")
    return _matmul([hid], [w2], [(0, 0)], w2.shape[1], "resid", extras=(x, gf), out_dtype=F32,
                   tk_pref=4096, name="ffn_down")


def _even_mixer(h_lat, h_ctx, x_lat, x_ctx, gm_lat, gm_ctx, tables, w_in, conv_w, conv_b, ln_g, ln_b,
                q_g, k_g, lamv, subg, w_out, layer_idx, need_ctx):
    d_model = h_lat.shape[1]
    conv_ch = conv_w.shape[1]
    diff_w = d_model - conv_ch
    n_heads = diff_w // (2 * HEAD_DIM)
    lam_init = 0.8 - 0.6 * math.exp(-0.3 * layer_idx)
    q_off = 2 * conv_ch
    g_vec = jnp.concatenate([jnp.tile(q_g * QK_SCALE_LOG2E, diff_w // HEAD_DIM),
                             jnp.tile(k_g, diff_w // HEAD_DIM)]).reshape(1, -1)

    def project(h, rope_tabs):
        u = _matmul([h], [w_in, w_in], [(0, 0), (0, conv_ch)], conv_ch, "glu", out_dtype=F32,
                    tn_pref=512, name="in_glu")
        qk = _matmul([h], [w_in], [(0, q_off)], 2 * diff_w, "qk", extras=(g_vec,) + tuple(rope_tabs),
                     rope=bool(rope_tabs), name="in_qk")
        vt = _matmul([h], [w_in], [(0, q_off + 2 * diff_w)], diff_w, "plain_t", name="in_v")
        return u, qk, vt

    def finish(u, o, x, gm):
        a = _conv_branch(u, conv_w, conv_b, ln_g, ln_b)
        return _matmul([a, o], [w_out, w_out], [(0, 0), (conv_ch, 0)], d_model, "resid",
                       extras=(x, gm), out_dtype=F32, name="out_proj")

    def attend(q_arr, sources):
        return _attention(q_arr, 0, sources, n_heads, 2, False, 2 * HEAD_DIM, "diff",
                          extras=(lamv, subg.reshape(1, -1)), lam_init=lam_init)

    u_lat, qk_lat, vt_lat = project(h_lat, tables)
    u_ctx, qk_ctx, vt_ctx = project(h_ctx, ())
    o_lat = attend(qk_lat, [(qk_lat, diff_w, vt_lat), (qk_ctx, diff_w, vt_ctx)])
    x_lat = finish(u_lat, o_lat, x_lat, gm_lat)
    if need_ctx:
        o_ctx = attend(qk_ctx, [(qk_ctx, diff_w, vt_ctx)])
        x_ctx = finish(u_ctx, o_ctx, x_ctx, gm_ctx)
    return x_lat, x_ctx


def _odd_mixer(h_lat, h_ctx, x_lat, x_ctx, gm_lat, gm_ctx, tables, w_in, q_g, k_g, w_out, need_ctx):
    d_model = h_lat.shape[1]
    n_q = d_model // HEAD_DIM
    n_kv = (w_in.shape[1] // HEAD_DIM - n_q) // 2
    group = n_q // n_kv
    q_w, kv_w = n_q * HEAD_DIM, n_kv * HEAD_DIM
    g_vec = jnp.concatenate([jnp.tile(q_g * QK_SCALE_LOG2E, n_q), jnp.tile(k_g, n_kv)]).reshape(1, -1)

    def project(h, rope_tabs, with_q):
        c0 = 0 if with_q else q_w
        qk = _matmul([h], [w_in], [(0, c0)], q_w + kv_w - c0, "qk", extras=(g_vec[:, c0:],) + tuple(rope_tabs),
                     rope=bool(rope_tabs), name="in_qk")
        vt = _matmul([h], [w_in], [(0, q_w + kv_w)], kv_w, "plain_t", name="in_v")
        return qk, vt

    def attend(q_arr, sources):
        return _attention(q_arr, 0, sources, n_kv, group, True, HEAD_DIM, "gqa")

    qk_lat, vt_lat = project(h_lat, tables, True)
    qk_ctx, vt_ctx = project(h_ctx, (), need_ctx)
    k_ctx_col = q_w if need_ctx else 0
    o_lat = attend(qk_lat, [(qk_lat, q_w, vt_lat), (qk_ctx, k_ctx_col, vt_ctx)])
    x_lat = _matmul([o_lat], [w_out], [(0, 0)], d_model, "resid", extras=(x_lat, gm_lat), out_dtype=F32,
                    name="out_proj")
    if need_ctx:
        o_ctx = attend(qk_ctx, [(qk_ctx, q_w, vt_ctx)])
        x_ctx = _matmul([o_ctx], [w_out], [(0, 0)], d_model, "resid", extras=(x_ctx, gm_ctx), out_dtype=F32,
                        name="out_proj")
    return x_lat, x_ctx


def kernel(x, c, ctx, c_ctx, w_ada, b_ada, norm_mix_g, norm_ffn_g, even_w_in, even_conv_w, even_conv_b,
           even_ln_g, even_ln_b, even_q_norm_g, even_k_norm_g, even_lambda_q1, even_lambda_k1,
           even_lambda_q2, even_lambda_k2, even_subln_g, even_w_out, odd_w_in, odd_q_norm_g, odd_k_norm_g,
           odd_w_out, w_ffn1, w_ffn2):
    batch, seq, d_model = x.shape
    depth = w_ada.shape[0]
    assert seq % GRID_W == 0
    tables = _rope_tables(seq // GRID_W)

    mods = _ada_mod(jnp.concatenate([c, c_ctx[None, :]], axis=0), w_ada, b_ada)
    mods = mods.reshape(depth, SUBLANES, 6, 1, d_model)

    def per_layer(w):
        w = w.astype(BF16)
        return [LayerWeight(w, i) for i in range(w.shape[0])]

    even_w_in, even_w_out, odd_w_in, odd_w_out = map(per_layer, (even_w_in, even_w_out, odd_w_in, odd_w_out))
    w_ffn1, w_ffn2 = per_layer(w_ffn1), per_layer(w_ffn2)

    outs = []
    for b in range(batch):
        x_lat, x_ctx = x[b], ctx[b]
        for l in range(depth):
            need_ctx = l < depth - 1
            sm, cm, gm, sf, cf, gf = (mods[l, b, t] for t in range(6))
            sm_c, cm_c, gm_c, sf_c, cf_c, gf_c = (mods[l, batch, t] for t in range(6))
            h_lat = _norm_mod(x_lat, norm_mix_g[l], cm, sm)
            h_ctx = _norm_mod(x_ctx, norm_mix_g[l], cm_c, sm_c)
            if l % 2 == 0:
                e = l // 2
                lamv = jnp.stack([even_lambda_q1[e], even_lambda_k1[e], even_lambda_q2[e], even_lambda_k2[e]])
                x_lat, x_ctx_new = _even_mixer(
                    h_lat, h_ctx, x_lat, x_ctx, gm, gm_c, tables, even_w_in[e], even_conv_w[e], even_conv_b[e],
                    even_ln_g[e], even_ln_b[e], even_q_norm_g[e], even_k_norm_g[e], lamv, even_subln_g[e],
                    even_w_out[e], l, need_ctx)
            else:
                o = l // 2
                x_lat, x_ctx_new = _odd_mixer(h_lat, h_ctx, x_lat, x_ctx, gm, gm_c, tables, odd_w_in[o],
                                              odd_q_norm_g[o], odd_k_norm_g[o], odd_w_out[o], need_ctx)
            x_lat = _ffn(x_lat, (sf, cf, gf), norm_ffn_g[l], w_ffn1[l], w_ffn2[l])
            if need_ctx:
                x_ctx = _ffn(x_ctx_new, (sf_c, cf_c, gf_c), norm_ffn_g[l], w_ffn1[l], w_ffn2[l])
        outs.append(x_lat)
    return jnp.stack(outs, axis=0)
```

```python
import functools
import math
from typing import NamedTuple

import jax
import jax.numpy as jnp
from jax import lax
from jax.experimental import pallas as pl
from jax.experimental.pallas import tpu as pltpu

HEAD_DIM = 128
GRID_W = 64
ROPE_THETA = 10000.0
EPS = 1e-6

LANES = 128
MXU_COLS = 256
QK_ROW_GROUP = 256
SUBLANES = 8
PACKED_ROWS = 16
V7X_VMEM_BYTES = 64 * 2**20
VMEM_CAP_BYTES = V7X_VMEM_BYTES - 6 * 2**20
CONV_HALO = 16
ATTN_LOOKAHEAD = 3

F32 = jnp.float32
BF16 = jnp.bfloat16
ATTN_EXP_DTYPE = F32
LOG2E = math.log2(math.e)
QK_SCALE_LOG2E = (HEAD_DIM ** -0.5) * LOG2E


def _tile(n, pref, mult):
    if n <= pref:
        return n
    t = (pref // mult) * mult
    while t >= mult:
        if n % t == 0:
            return t
        t -= mult
    raise ValueError(f"no tile for {n} (pref {pref}, mult {mult})")


def _params(n_axes, est_bytes):
    limit = int(min(max(est_bytes * 1.25 + 4 * 2**20, 16 * 2**20), VMEM_CAP_BYTES))
    return pltpu.CompilerParams(dimension_semantics=("arbitrary",) * n_axes, vmem_limit_bytes=limit)


def _nbytes(shape, dtype):
    return math.prod(shape) * jnp.dtype(dtype).itemsize


def _ada_kernel(a_ref, w_ref, b_ref, o_ref, act_ref):
    nv = a_ref.shape[0]
    tn = w_ref.shape[1]

    @pl.when((pl.program_id(0) == 0) & (pl.program_id(1) == 0))
    def _():
        a = a_ref[...]
        act_ref[...] = a * jax.nn.sigmoid(a)

    rows = []
    for v in range(nv):
        parts = []
        for t in range(tn // LANES):
            prod = w_ref[:, t * LANES:(t + 1) * LANES] * act_ref[v]
            parts.append(jnp.sum(prod, axis=0, keepdims=True))
        rows.append(jnp.concatenate(parts, axis=1) + b_ref[...])
    rows.append(jnp.zeros((SUBLANES - nv, tn), F32))
    o_ref[...] = jnp.concatenate(rows, axis=0)


def _ada_mod(cvecs, w_ada, b_ada):
    nv, d = cvecs.shape
    depth, _, n = w_ada.shape
    assert nv < SUBLANES
    tn = _tile(n, 512, LANES)
    a_rep = jnp.broadcast_to(cvecs[:, :, None], (nv, d, LANES))
    est = 2 * (_nbytes((d, tn), F32) + _nbytes((nv, d, LANES), F32)) + _nbytes((nv, d, LANES), F32)
    return pl.pallas_call(
        _ada_kernel,
        out_shape=jax.ShapeDtypeStruct((depth, SUBLANES, n), F32),
        grid=(depth, n // tn),
        in_specs=[
            pl.BlockSpec((nv, d, LANES), lambda l, j: (0, 0, 0)),
            pl.BlockSpec((None, d, tn), lambda l, j: (l, 0, j)),
            pl.BlockSpec((None, 1, tn), lambda l, j: (l, 0, j)),
        ],
        out_specs=pl.BlockSpec((None, SUBLANES, tn), lambda l, j: (l, 0, j)),
        scratch_shapes=[pltpu.VMEM((nv, d, LANES), F32)],
        compiler_params=_params(2, est),
        name="ada_mod",
    )(a_rep, w_ada, b_ada.reshape(depth, 1, n))


def _norm_mod_kernel(x_ref, g_ref, scale_ref, shift_ref, o_ref):
    x = x_ref[...]
    ms = jnp.mean(x * x, axis=-1, keepdims=True)
    y = x * lax.rsqrt(ms + EPS) * g_ref[...]
    o_ref[...] = (y * (1.0 + scale_ref[...]) + shift_ref[...]).astype(o_ref.dtype)


def _norm_mod(x, g, scale, shift):
    m, d = x.shape
    tm = _tile(m, 256, SUBLANES)
    vec = pl.BlockSpec((1, d), lambda i: (0, 0))
    est = 2 * (_nbytes((tm, d), F32) + _nbytes((tm, d), BF16)) + 2 * _nbytes((tm, d), F32)
    return pl.pallas_call(
        _norm_mod_kernel,
        out_shape=jax.ShapeDtypeStruct((m, d), BF16),
        grid=(m // tm,),
        in_specs=[pl.BlockSpec((tm, d), lambda i: (i, 0)), vec, vec, vec],
        out_specs=pl.BlockSpec((tm, d), lambda i: (i, 0)),
        compiler_params=_params(1, est),
        name="norm_mod",
    )(x, g.reshape(1, d), scale, shift)


def _dot(a, b):
    return jnp.dot(a, b, preferred_element_type=F32)


def _dot_nt(a, b):
    return lax.dot_general(a, b, (((1,), (1,)), ((), ())), preferred_element_type=F32)


def _rope_rotate(y, cos, sin_up, sin_dn):
    return y * cos + pltpu.roll(y, 96, 1) * sin_up + pltpu.roll(y, 32, 1) * sin_dn


def _mm_kernel(*refs, n_a, n_w, nk, epilogue, rope):
    a_refs = refs[:n_a]
    w_refs = refs[n_a:n_a + n_w]
    rest = refs[n_a + n_w:]

    if epilogue == "glu":
        val = _dot(a_refs[0][...], w_refs[0][...])
        gate = _dot(a_refs[0][...], w_refs[1][...])
        o_ref = rest[0]
        o_ref[...] = val * jax.nn.sigmoid(gate)
        return

    def partial_sum(cols, rows=slice(None)):
        acc = _dot(a_refs[0][rows, :], w_refs[0][:, cols])
        for p in range(1, n_a):
            acc = acc + _dot(a_refs[p][rows, :], w_refs[p][:, cols])
        return acc

    tm, tn = a_refs[0].shape[0], w_refs[0].shape[1]
    groups = [slice(c0, c0 + MXU_COLS) for c0 in range(0, tn, MXU_COLS)] if tn % MXU_COLS == 0 else [slice(0, tn)]

    if epilogue == "qk":
        if rope:
            g_ref, cos_ref, sup_ref, sdn_ref, o_ref = rest
        else:
            g_ref, o_ref = rest
        rg = _tile(tm, QK_ROW_GROUP, SUBLANES)
        for r0 in range(0, tm, rg):
            rows = slice(r0, r0 + rg)
            acc = partial_sum(slice(0, tn), rows)
            for h0 in range(0, tn, HEAD_DIM):
                sl = slice(h0, h0 + HEAD_DIM)
                blk = acc[:, sl]
                ms = jnp.mean(blk * blk, axis=-1, keepdims=True)
                y = blk * lax.rsqrt(ms + EPS) * g_ref[:, sl]
                if rope:
                    y = _rope_rotate(y, cos_ref[rows, :], sup_ref[rows, :], sdn_ref[rows, :])
                o_ref[rows, sl] = y.astype(o_ref.dtype)
        return

    if epilogue == "resid" and nk > 1:
        x_ref, gate_ref, o_ref, acc_ref = rest
        k = pl.program_id(2)

        @pl.when(k == 0)
        def _():
            acc_ref[...] = jnp.zeros_like(acc_ref)

        for cols in groups:
            acc_ref[:, cols] += partial_sum(cols)

        @pl.when(k == nk - 1)
        def _():
            o_ref[...] = x_ref[...] + gate_ref[...] * acc_ref[...]
        return

    acc = partial_sum(slice(0, tn))

    if epilogue == "plain":
        rest[0][...] = acc.astype(rest[0].dtype)
    elif epilogue == "plain_t":
        rest[0][...] = acc.T.astype(rest[0].dtype)
    elif epilogue == "sqrelu":
        r = jnp.maximum(acc, 0.0)
        rest[0][...] = (r * r).astype(rest[0].dtype)
    elif epilogue == "resid":
        x_ref, gate_ref, o_ref = rest
        o_ref[...] = x_ref[...] + gate_ref[...] * acc
    else:
        raise ValueError(epilogue)


class LayerWeight(NamedTuple):
    stack: jax.Array
    layer: int

    @property
    def shape(self):
        return self.stack.shape[1:]


def _matmul(a_list, w_list, w_offsets, n_out, epilogue, extras=(), rope=False, out_dtype=BF16,
            tm_pref=1024, tn_pref=1024, tk_pref=4096, name="mm"):
    m, kdim = a_list[0].shape
    n_a, n_w = len(a_list), len(w_list)
    tm = _tile(m, tm_pref, SUBLANES)
    tn = _tile(n_out, tn_pref, LANES)
    tk = _tile(kdim, tk_pref, LANES)
    nk = kdim // tk
    assert nk == 1 or epilogue == "resid"
    grid = (m // tm, n_out // tn, nk)

    in_specs, est = [], 0
    for _ in a_list:
        in_specs.append(pl.BlockSpec((tm, tk), lambda i, j, k: (i, k)))
        est += 2 * _nbytes((tm, tk), BF16)
    for w, (ro, co) in zip(w_list, w_offsets):
        assert ro % tk == 0 and co % tn == 0
        in_specs.append(pl.BlockSpec((None, tk, tn), functools.partial(
            lambda i, j, k, layer, rb, cb: (layer, k + rb, j + cb), layer=w.layer, rb=ro // tk, cb=co // tn)))
        est += 2 * _nbytes((tk, tn), BF16)
    operands = list(a_list) + [w.stack for w in w_list]

    col_vec = pl.BlockSpec((1, tn), lambda i, j, k: (0, j))
    row_tab = pl.BlockSpec((tm, HEAD_DIM), lambda i, j, k: (i, 0))
    tile = pl.BlockSpec((tm, tn), lambda i, j, k: (i, j))
    scratch = []
    if epilogue == "qk":
        in_specs += [col_vec] + ([row_tab] * 3 if rope else [])
        est += 6 * _nbytes((tm, HEAD_DIM), F32)
    elif epilogue == "resid":
        in_specs += [tile, col_vec]
        est += 2 * _nbytes((tm, tn), F32)
        if nk > 1:
            scratch.append(pltpu.VMEM((tm, tn), F32))
            est += _nbytes((tm, tn), F32)
    operands += list(extras)
    est += 2 * _nbytes((tm, tn), out_dtype) + 3 * _nbytes((tm, tn), F32)

    out_shape, out_spec = (m, n_out), tile
    if epilogue == "plain_t":
        out_shape, out_spec = (n_out, m), pl.BlockSpec((tn, tm), lambda i, j, k: (j, i))
    return pl.pallas_call(
        functools.partial(_mm_kernel, n_a=n_a, n_w=n_w, nk=nk, epilogue=epilogue, rope=rope),
        out_shape=jax.ShapeDtypeStruct(out_shape, out_dtype),
        grid=grid,
        in_specs=in_specs,
        out_specs=out_spec,
        scratch_shapes=scratch,
        compiler_params=_params(3, est),
        name=name,
    )(*operands)


def _conv_kernel(prev_ref, cur_ref, next_ref, w_ref, cb_ref, g_ref, b_ref, o_ref, ext_ref, y_ref, sh_ref,
                 *, width, rb, cw):
    tm, ch = cur_ref.shape
    half = (width - 1) // 2
    sh_rows = sh_ref.shape[1]
    i = pl.program_id(0)
    n = pl.num_programs(0)
    ext_ref[0:CONV_HALO, :] = jnp.where(i > 0, prev_ref[...], 0.0)
    ext_ref[CONV_HALO:CONV_HALO + tm, :] = cur_ref[...]
    ext_ref[CONV_HALO + tm:, :] = jnp.where(i < n - 1, next_ref[...], 0.0)

    def col_body(c, carry):
        c0 = pl.multiple_of(c * cw, cw)
        cols = pl.ds(c0, cw)
        taps = [w_ref[t:t + 1, cols] for t in range(width)]
        bias = cb_ref[:, cols]
        for rho in range(SUBLANES):
            sh_ref[rho] = ext_ref[pl.ds(rho, sh_rows), cols]
        for r in range(tm // rb):
            acc = jnp.zeros((rb, cw), F32) + bias
            for t in range(width):
                s = CONV_HALO - half + t
                a0 = r * rb + (s // SUBLANES) * SUBLANES
                acc = acc + sh_ref[s % SUBLANES, a0:a0 + rb, :] * taps[t]
            y_ref[r * rb:(r + 1) * rb, cols] = acc
        return carry

    lax.fori_loop(0, ch // cw, col_body, 0)

    y = y_ref[...]
    mu = jnp.mean(y, axis=-1, keepdims=True)
    yc = y - mu
    var = jnp.mean(yc * yc, axis=-1, keepdims=True)
    z = yc * lax.rsqrt(var + EPS) * g_ref[...] + b_ref[...]
    o_ref[...] = (z * jax.nn.sigmoid(z)).astype(o_ref.dtype)


def _conv_branch(u, conv_w, conv_b, ln_g, ln_b):
    m, ch = u.shape
    width = conv_w.shape[0]
    assert (width - 1) // 2 <= CONV_HALO and m % CONV_HALO == 0
    tm = _tile(m, 256, CONV_HALO)
    hb = tm // CONV_HALO
    last = m // CONV_HALO - 1
    vec = pl.BlockSpec((1, ch), lambda i: (0, 0))
    est = (2 * (_nbytes((tm + 2 * CONV_HALO, ch), F32) + _nbytes((tm, ch), BF16))
           + _nbytes((tm + 2 * CONV_HALO, ch), F32) + 4 * _nbytes((tm, ch), F32))
    return pl.pallas_call(
        functools.partial(_conv_kernel, width=width, rb=_tile(tm, 64, SUBLANES), cw=LANES),
        out_shape=jax.ShapeDtypeStruct((m, ch), BF16),
        grid=(m // tm,),
        in_specs=[
            pl.BlockSpec((CONV_HALO, ch), lambda i: (jnp.maximum(i * hb - 1, 0), 0)),
            pl.BlockSpec((tm, ch), lambda i: (i, 0)),
            pl.BlockSpec((CONV_HALO, ch), lambda i: (jnp.minimum((i + 1) * hb, last), 0)),
            pl.BlockSpec((width, ch), lambda i: (0, 0)),
            vec, vec, vec,
        ],
        out_specs=pl.BlockSpec((tm, ch), lambda i: (i, 0)),
        scratch_shapes=[pltpu.VMEM((tm + 2 * CONV_HALO, ch), F32), pltpu.VMEM((tm, ch), F32),
                        pltpu.VMEM((SUBLANES, tm + 2 * CONV_HALO - SUBLANES, LANES), F32)],
        compiler_params=_params(1, est),
        name="conv_branch",
    )(u, u, u, conv_w, conv_b.reshape(1, ch), ln_g.reshape(1, ch), ln_b.reshape(1, ch))


def _attn_kernel(*refs, n_src, chunks, n_chain, k_shared, dv, mode, lam_init):
    q_ref = refs[0]
    kv_refs = refs[1:1 + 2 * n_src]
    rest = refs[1 + 2 * n_src:]
    o_ref, acc_ref, sbuf_ref = rest[-3], rest[-2], rest[-1]
    d = HEAD_DIM
    tq = q_ref.shape[0]
    qs =[q_ref[:, g * d:(g + 1) * d] for g in range(n_chain)]
    acc_ref[...] = jnp.zeros_like(acc_ref)
    state = (jnp.full((1, tq), -jnp.inf, F32),) * n_chain

    for s_idx in range(n_src):
        k_ref, vt_ref = kv_refs[2 * s_idx], kv_refs[2 * s_idx + 1]
        tk = chunks[s_idx]
        n_chunks = k_ref.shape[0] // tk
        unroll = 2 if n_chunks % 2 == 0 else 1
        n_iter = n_chunks // unroll
        stages = [(u, g) for u in range(unroll) for g in range(n_chain)]
        n_st = len(stages)
        la = min(ATTN_LOOKAHEAD, n_st - 1 if n_iter > 1 else n_st)

        def rows_of(it, u, tk=tk, unroll=unroll):
            start = (it * unroll + u) * tk
            return pl.ds(start if isinstance(start, int) else pl.multiple_of(start, tk), tk)

        def scores(it, u, g, k_ref=k_ref):
            cols = slice(0, d) if k_shared else slice(g * d, (g + 1) * d)
            return _dot_nt(k_ref[rows_of(it, u), cols], qs[g])

        def body(ci, st, feed_next=True, vt_ref=vt_ref, tk=tk, n_iter=n_iter, stages=stages, n_st=n_st,
                 la=la, rows_of=rows_of, scores=scores):
            st = list(st)
            looped = n_iter > 1
            pending = [None] * la if looped else [scores(ci, *stages[j]) for j in range(la)]
            for t, (u, g) in enumerate(stages):
                if t + la < n_st:
                    pending.append(scores(ci, *stages[t + la]))
                elif looped and feed_next:
                    j = t + la - n_st
                    sbuf_ref[j, 0:tk, :] = scores(ci + 1, *stages[j])
                s = pending.pop(0)
                if s is None:
                    s = sbuf_ref[t, 0:tk, :]
                m = st[g]
                m_new = jnp.maximum(m, jnp.max(s, axis=0, keepdims=True))
                alpha = jnp.exp2(m - m_new)
                p = jnp.exp2((s - m_new).astype(ATTN_EXP_DTYPE)).astype(vt_ref.dtype)
                st[g] = m_new
                vt1 = jnp.concatenate([vt_ref[:, rows_of(ci, u)], jnp.ones((PACKED_ROWS, tk), vt_ref.dtype)], axis=0)
                acc_ref[g] = alpha * acc_ref[g] + _dot(vt1, p)
            return tuple(st)

        if n_iter > 1:
            for j in range(la):
                sbuf_ref[j, 0:tk, :] = scores(0, *stages[j])
            state = lax.fori_loop(0, n_iter - 1, body, state)
            state = body(n_iter - 1, state, feed_next=False)
        else:
            state = body(0, state)

    if mode == "gqa":
        for g in range(n_chain):
            o = (acc_ref[g, 0:dv, :] / acc_ref[g, dv:dv + 1, :]).T
            o_ref[:, g * dv:(g + 1) * dv] = o.astype(o_ref.dtype)
    else:
        lamv_ref, subg_ref = rest[0], rest[1]
        lv = lamv_ref[...]
        lam = (jnp.exp(jnp.sum(lv[0:1] * lv[1:2], axis=-1, keepdims=True))
               - jnp.exp(jnp.sum(lv[2:3] * lv[3:4], axis=-1, keepdims=True)) + lam_init)
        o = (acc_ref[0, 0:dv, :] / acc_ref[0, dv:dv + 1, :]
             - lam * (acc_ref[1, 0:dv, :] / acc_ref[1, dv:dv + 1, :])).T
        ms = jnp.mean(o * o, axis=-1, keepdims=True)
        o = o * lax.rsqrt(ms + EPS) * subg_ref[...] * (1.0 - lam_init)
        o_ref[...] = o.astype(o_ref.dtype)


def _attention(q_arr, q_col0, sources, n_heads, n_chain, k_shared, dv, mode, extras=(), lam_init=0.0,
               tq_pref=512, tk_pref=1024):
    lq = q_arr.shape[0]
    d = HEAD_DIM
    qw = n_chain * d
    kw = d if k_shared else qw
    ow = n_chain * dv if mode == "gqa" else dv
    tq = _tile(lq, tq_pref, LANES)
    in_specs = [pl.BlockSpec((tq, qw), functools.partial(lambda h, i, b: (i, h + b), b=q_col0 // qw))]
    operands, chunks = [q_arr], []
    est = 4 * _nbytes((tq, qw), BF16) + 2 * _nbytes((tq, ow), BF16) + _nbytes((n_chain, dv + PACKED_ROWS, tq), F32)
    for (k_arr, k_col0, vt_arr) in sources:
        lk = k_arr.shape[0]
        in_specs.append(pl.BlockSpec((lk, kw), functools.partial(lambda h, i, b: (0, h + b), b=k_col0 // kw)))
        in_specs.append(pl.BlockSpec((dv, lk), lambda h, i: (h, 0)))
        operands += [k_arr, vt_arr]
        tk = _tile(lk, tk_pref, LANES)
        chunks.append(tk)
        est += 2 * (_nbytes((lk, kw), BF16) + _nbytes((dv, lk), BF16)) + 4 * n_chain * _nbytes((tk, tq), F32)
    for e in extras:
        in_specs.append(pl.BlockSpec(e.shape, lambda h, i: (0, 0)))
    operands += list(extras)
    return pl.pallas_call(
        functools.partial(_attn_kernel, n_src=len(sources), chunks=tuple(chunks), n_chain=n_chain,
                          k_shared=k_shared, dv=dv, mode=mode, lam_init=lam_init),
        out_shape=jax.ShapeDtypeStruct((lq, n_heads * ow), BF16),
        grid=(n_heads, lq // tq),
        in_specs=in_specs,
        out_specs=pl.BlockSpec((tq, ow), lambda h, i: (i, h)),
        scratch_shapes=[pltpu.VMEM((n_chain, dv + PACKED_ROWS, tq), F32),
                        pltpu.VMEM((ATTN_LOOKAHEAD, max(chunks), tq), F32)],
        compiler_params=_params(2, est + _nbytes((ATTN_LOOKAHEAD, max(chunks), tq), F32)),
        name=mode + "_attn",
    )(*operands)


def _rope_tables(rows):
    n_freq = HEAD_DIM // 4
    tok = jnp.arange(rows * GRID_W, dtype=jnp.int32)
    row = (tok // GRID_W).astype(F32)
    col = (tok % GRID_W).astype(F32)
    inv = ROPE_THETA ** (-jnp.arange(n_freq, dtype=F32) / n_freq)
    cr, sr = jnp.cos(row[:, None] * inv), jnp.sin(row[:, None] * inv)
    cc, sc = jnp.cos(col[:, None] * inv), jnp.sin(col[:, None] * inv)
    z = jnp.zeros_like(sr)
    cos = jnp.concatenate([cr, cr, cc, cc], axis=-1)
    sin_up = jnp.concatenate([-sr, z, -sc, z], axis=-1)
    sin_dn = jnp.concatenate([z, sr, z, sc], axis=-1)
    return cos, sin_up, sin_dn


def _ffn(x, mods, g, w1, w2):
    sf, cf, gf = mods
    h = _norm_mod(x, g, cf, sf)
    hid = _matmul([h], [w1], [(0, 0)], w1.shape[1], "sqrelu", name="ffn_up")
    return _matmul([hid], [w2], [(0, 0)], w2.shape[1], "resid", extras=(x, gf), out_dtype=F32,
                   tk_pref=4096, name="ffn_down")


def _even_mixer(h_lat, h_ctx, x_lat, x_ctx, gm_lat, gm_ctx, tables, w_in, conv_w, conv_b, ln_g, ln_b,
                q_g, k_g, lamv, subg, w_out, layer_idx, need_ctx):
    d_model = h_lat.shape[1]
    conv_ch = conv_w.shape[1]
    diff_w = d_model - conv_ch
    n_heads = diff_w // (2 * HEAD_DIM)
    lam_init = 0.8 - 0.6 * math.exp(-0.3 * layer_idx)
    q_off = 2 * conv_ch
    g_vec = jnp.concatenate([jnp.tile(q_g * QK_SCALE_LOG2E, diff_w // HEAD_DIM),
                             jnp.tile(k_g, diff_w // HEAD_DIM)]).reshape(1, -1)

    def project(h, rope_tabs):
        u = _matmul([h], [w_in, w_in], [(0, 0), (0, conv_ch)], conv_ch, "glu", out_dtype=F32,
                    tn_pref=512, name="in_glu")
        qk = _matmul([h], [w_in], [(0, q_off)], 2 * diff_w, "qk", extras=(g_vec,) + tuple(rope_tabs),
                     rope=bool(rope_tabs), name="in_qk")
        vt = _matmul([h], [w_in], [(0, q_off + 2 * diff_w)], diff_w, "plain_t", name="in_v")
        return u, qk, vt

    def finish(u, o, x, gm):
        a = _conv_branch(u, conv_w, conv_b, ln_g, ln_b)
        return _matmul([a, o], [w_out, w_out], [(0, 0), (conv_ch, 0)], d_model, "resid",
                       extras=(x, gm), out_dtype=F32, name="out_proj")

    def attend(q_arr, sources):
        return _attention(q_arr, 0, sources, n_heads, 2, False, 2 * HEAD_DIM, "diff",
                          extras=(lamv, subg.reshape(1, -1)), lam_init=lam_init)

    u_lat, qk_lat, vt_lat = project(h_lat, tables)
    u_ctx, qk_ctx, vt_ctx = project(h_ctx, ())
    o_lat = attend(qk_lat, [(qk_lat, diff_w, vt_lat), (qk_ctx, diff_w, vt_ctx)])
    x_lat = finish(u_lat, o_lat, x_lat, gm_lat)
    if need_ctx:
        o_ctx = attend(qk_ctx, [(qk_ctx, diff_w, vt_ctx)])
        x_ctx = finish(u_ctx, o_ctx, x_ctx, gm_ctx)
    return x_lat, x_ctx


def _odd_mixer(h_lat, h_ctx, x_lat, x_ctx, gm_lat, gm_ctx, tables, w_in, q_g, k_g, w_out, need_ctx):
    d_model = h_lat.shape[1]
    n_q = d_model // HEAD_DIM
    n_kv = (w_in.shape[1] // HEAD_DIM - n_q) // 2
    group = n_q // n_kv
    q_w, kv_w = n_q * HEAD_DIM, n_kv * HEAD_DIM
    g_vec = jnp.concatenate([jnp.tile(q_g * QK_SCALE_LOG2E, n_q), jnp.tile(k_g, n_kv)]).reshape(1, -1)

    def project(h, rope_tabs, with_q):
        c0 = 0 if with_q else q_w
        qk = _matmul([h], [w_in], [(0, c0)], q_w + kv_w - c0, "qk", extras=(g_vec[:, c0:],) + tuple(rope_tabs),
                     rope=bool(rope_tabs), name="in_qk")
        vt = _matmul([h], [w_in], [(0, q_w + kv_w)], kv_w, "plain_t", name="in_v")
        return qk, vt

    def attend(q_arr, sources):
        return _attention(q_arr, 0, sources, n_kv, group, True, HEAD_DIM, "gqa")

    qk_lat, vt_lat = project(h_lat, tables, True)
    qk_ctx, vt_ctx = project(h_ctx, (), need_ctx)
    k_ctx_col = q_w if need_ctx else 0
    o_lat = attend(qk_lat, [(qk_lat, q_w, vt_lat), (qk_ctx, k_ctx_col, vt_ctx)])
    x_lat = _matmul([o_lat], [w_out], [(0, 0)], d_model, "resid", extras=(x_lat, gm_lat), out_dtype=F32,
                    name="out_proj")
    if need_ctx:
        o_ctx = attend(qk_ctx, [(qk_ctx, q_w, vt_ctx)])
        x_ctx = _matmul([o_ctx], [w_out], [(0, 0)], d_model, "resid", extras=(x_ctx, gm_ctx), out_dtype=F32,
                        name="out_proj")
    return x_lat, x_ctx


def kernel(x, c, ctx, c_ctx, w_ada, b_ada, norm_mix_g, norm_ffn_g, even_w_in, even_conv_w, even_conv_b,
           even_ln_g, even_ln_b, even_q_norm_g, even_k_norm_g, even_lambda_q1, even_lambda_k1,
           even_lambda_q2, even_lambda_k2, even_subln_g, even_w_out, odd_w_in, odd_q_norm_g, odd_k_norm_g,
           odd_w_out, w_ffn1, w_ffn2):
    batch, seq, d_model = x.shape
    depth = w_ada.shape[0]
    assert seq % GRID_W == 0
    tables = _rope_tables(seq // GRID_W)

    mods = _ada_mod(jnp.concatenate([c, c_ctx[None, :]], axis=0), w_ada, b_ada)
    mods = mods.reshape(depth, SUBLANES, 6, 1, d_model)

    def per_layer(w):
        w = w.astype(BF16)
        return [LayerWeight(w, i) for i in range(w.shape[0])]

    even_w_in, even_w_out, odd_w_in, odd_w_out = map(per_layer, (even_w_in, even_w_out, odd_w_in, odd_w_out))
    w_ffn1, w_ffn2 = per_layer(w_ffn1), per_layer(w_ffn2)

    outs = []
    for b in range(batch):
        x_lat, x_ctx = x[b], ctx[b]
        for l in range(depth):
            need_ctx = l < depth - 1
            sm, cm, gm, sf, cf, gf = (mods[l, b, t] for t in range(6))
            sm_c, cm_c, gm_c, sf_c, cf_c, gf_c = (mods[l, batch, t] for t in range(6))
            h_lat = _norm_mod(x_lat, norm_mix_g[l], cm, sm)
            h_ctx = _norm_mod(x_ctx, norm_mix_g[l], cm_c, sm_c)
            if l % 2 == 0:
                e = l // 2
                lamv = jnp.stack([even_lambda_q1[e], even_lambda_k1[e], even_lambda_q2[e], even_lambda_k2[e]])
                x_lat, x_ctx_new = _even_mixer(
                    h_lat, h_ctx, x_lat, x_ctx, gm, gm_c, tables, even_w_in[e], even_conv_w[e], even_conv_b[e],
                    even_ln_g[e], even_ln_b[e], even_q_norm_g[e], even_k_norm_g[e], lamv, even_subln_g[e],
                    even_w_out[e], l, need_ctx)
            else:
                o = l // 2
                x_lat, x_ctx_new = _odd_mixer(h_lat, h_ctx, x_lat, x_ctx, gm, gm_c, tables, odd_w_in[o],
                                              odd_q_norm_g[o], odd_k_norm_g[o], odd_w_out[o], need_ctx)
            x_lat = _ffn(x_lat, (sf, cf, gf), norm_ffn_g[l], w_ffn1[l], w_ffn2[l])
            if need_ctx:
                x_ctx = _ffn(x_ctx_new, (sf_c, cf_c, gf_c), norm_ffn_g[l], w_ffn1[l], w_ffn2[l])
        outs.append(x_lat)
    return jnp.stack(outs, axis=0)
```
